```python
import math
import jax
import jax.numpy as jnp
from jax import lax
import numpy as np

D_MODEL = 4096
BATCH = 2
SEQ = 4096
DEPTH = 2

N_META = 16
CHUNK = 64
N_BRANCH = 4
MIX_W = D_MODEL // N_BRANCH
EPS = 1e-6

RET_HEADS = 4
RET_DK = MIX_W // RET_HEADS
RET_DV = MIX_W // RET_HEADS
ROPE_BASE = 10000.0
GLA_HEADS = 4
GLA_DK = MIX_W // (2 * GLA_HEADS)
GLA_DV = MIX_W // GLA_HEADS
GLA_RANK = 16
GLA_TAU = 16.0
GDN_HEADS = 8
GDN_DK = MIX_W // GDN_HEADS
GDN_DV = MIX_W // GDN_HEADS
GDN_CONV = 4
RWKV_HEAD = 64
RWKV_HEADS = MIX_W // RWKV_HEAD
RWKV_W_RANK = 64
RWKV_A_RANK = 64
RWKV_G_RANK = 160
RWKV_DECAY_SCALE = 0.606531
RWKV_GN_EPS = 64e-5
RWKV_SRC_W = 3 * MIX_W + RWKV_W_RANK + RWKV_A_RANK + RWKV_G_RANK
RWKV_SPLIT_IDX = tuple(int(i) for i in np.cumsum((MIX_W, MIX_W, MIX_W, RWKV_W_RANK, RWKV_A_RANK, RWKV_G_RANK))[:-1])
D_FF = 11008
FFN_CONV = 3

SPLIT_SIZES = (
    RET_HEADS * RET_DK, RET_HEADS * RET_DK, RET_HEADS * RET_DV, RET_HEADS * RET_DV,
    GLA_HEADS * GLA_DK, GLA_HEADS * GLA_DK, GLA_HEADS * GLA_DV, GLA_HEADS * GLA_DV, GLA_RANK,
    GDN_HEADS * GDN_DK, GDN_HEADS * GDN_DK, GDN_HEADS * GDN_DV, GDN_HEADS * GDN_DV,
    GDN_HEADS, GDN_HEADS,
    RWKV_SRC_W,
)
IN_W = sum(SPLIT_SIZES)
SPLIT_IDX = tuple(int(i) for i in np.cumsum(SPLIT_SIZES)[:-1])
F32 = jnp.float32

kernel_name = 'hybrid_gated_parallel_mixer_trunk'


def _rmsnorm(x, gain):
    xf = x.astype(F32)
    y = xf * lax.rsqrt(jnp.mean(xf * xf, axis=-1, keepdims=True) + EPS)
    return (y * gain.astype(F32)).astype(x.dtype)


def _head_norm(t, gain, eps=EPS, center=False):
    if center:
        t = t - jnp.mean(t, axis=-1, keepdims=True)
    return t * lax.rsqrt(jnp.mean(t * t, axis=-1, keepdims=True) + eps) * gain.astype(F32)


def _l2norm(t):
    return t * lax.rsqrt(jnp.sum(t * t, axis=-1, keepdims=True) + EPS)


def _heads(t, n_heads):
    return t.reshape(t.shape[:-1] + (n_heads, t.shape[-1] // n_heads))


def _token_shift(t):
    return jnp.pad(t, ((0, 0), (1, 0), (0, 0)))[:, :-1]


def _causal_dwconv(t, w):
    K, C = w.shape
    return lax.conv_general_dilated(t, w[:, None, :].astype(t.dtype), window_strides=(1,),
                                    padding=((K - 1, 0),), dimension_numbers=('NWC', 'WIO', 'NWC'),
                                    feature_group_count=C)


def _rope(t, pos):
    half = t.shape[-1] // 2
    inv_freq = ROPE_BASE ** (-jnp.arange(half, dtype=F32) / half)
    ang = pos.astype(F32)[:, None] * inv_freq[None, :]
    cos = jnp.cos(ang)[None, :, None, :]
    sin = jnp.sin(ang)[None, :, None, :]
    t1, t2 = t[..., :half], t[..., half:]
    return jnp.concatenate([t1 * cos - t2 * sin, t1 * sin + t2 * cos], axis=-1)


def _to_chunks(t):
    t = jnp.pad(t, ((0, 0), (CHUNK - N_META, 0), (0, 0), (0, 0)))
    B, T, H, d = t.shape
    return jnp.transpose(t.reshape(B, T // CHUNK, CHUNK, H, d), (1, 0, 3, 2, 4))


def _from_chunks(o):
    N, B, H, C, d = o.shape
    o = jnp.transpose(o, (1, 0, 3, 2, 4)).reshape(B, N * C, H, d)
    return o[:, CHUNK - N_META:]


def _retention_chunked(q, k, v, gamma):
    log_g = jnp.log(gamma)
    idx = jnp.arange(CHUNK, dtype=F32)
    rel = idx[:, None] - idx[None, :]
    decay = jnp.where(rel >= 0, jnp.exp(log_g[:, None, None] * jnp.maximum(rel, 0.0)), 0.0)
    q_decay = jnp.exp(log_g[:, None] * (idx + 1.0))[:, :, None]
    k_decay = jnp.exp(log_g[:, None] * (CHUNK - 1.0 - idx))[:, :, None]
    s_decay = jnp.exp(log_g * CHUNK)[:, None, None]

    def step(S, inp):
        qi, ki, vi = inp
        scores = jnp.einsum('bhid,bhjd->bhij', qi, ki) * decay
        o = jnp.einsum('bhij,bhjv->bhiv', scores, vi) + jnp.einsum('bhid,bhdv->bhiv', qi, S) * q_decay
        S = S * s_decay + jnp.einsum('bhjd,bhjv->bhdv', ki * k_decay, vi)
        return S, o

    B, _, H, dk = q.shape
    S0 = jnp.zeros((B, H, dk, v.shape[-1]), F32)
    _, o = lax.scan(step, S0, (_to_chunks(q), _to_chunks(k), _to_chunks(v)))
    return _from_chunks(o)


def _gla_chunked(q, k, v, log_a):
    B, _, H, dk = q.shape
    incl = jnp.tril(jnp.ones((CHUNK, CHUNK), bool))[:, :, None]

    def step(S, inp):
        qi, ki, vi, li = inp
        b = jnp.cumsum(li, axis=-2)
        diff = jnp.where(incl, b[:, :, :, None, :] - b[:, :, None, :, :], -jnp.inf)
        scores = jnp.einsum('bhid,bhjd,bhijd->bhij', qi, ki, jnp.exp(diff))
        o = jnp.einsum('bhij,bhjv->bhiv', scores, vi) + jnp.einsum('bhid,bhdv->bhiv', qi * jnp.exp(b), S)
        b_last = b[:, :, -1:, :]
        S = S * jnp.exp(b_last)[:, :, 0, :, None] + jnp.einsum('bhjd,bhjv->bhdv', ki * jnp.exp(b_last - b), vi)
        return S, o

    S0 = jnp.zeros((B, H, dk, v.shape[-1]), F32)
    _, o = lax.scan(step, S0, (_to_chunks(q), _to_chunks(k), _to_chunks(v), _to_chunks(log_a)))
    return _from_chunks(o)


def _gated_delta_chunked(q, k, v, g, beta):
    B, _, H, dk = q.shape
    dv = v.shape[-1]
    incl = jnp.tril(jnp.ones((CHUNK, CHUNK), bool))
    strict = jnp.tril(jnp.ones((CHUNK, CHUNK), bool), -1)
    eye = jnp.eye(CHUNK, dtype=F32)

    def step(S, inp):
        qi, ki, vi, gi, bi = inp
        gcum = jnp.cumsum(gi, axis=-1)
        decay = jnp.exp(jnp.where(incl, gcum[..., :, None] - gcum[..., None, :], -jnp.inf))
        kb = ki * bi[..., None]
        a_mat = eye + jnp.where(strict, jnp.einsum('bhid,bhjd->bhij', kb, ki) * decay, 0.0)
        rhs = jnp.concatenate([vi * bi[..., None], kb * jnp.exp(gcum)[..., None]], axis=-1)
        sol = lax.linalg.triangular_solve(a_mat, rhs, left_side=True, lower=True, unit_diagonal=True)
        u, w = sol[..., :dv], sol[..., dv:]
        v_new = u - jnp.einsum('bhcd,bhdv->bhcv', w, S)
        attn = jnp.einsum('bhid,bhjd->bhij', qi, ki) * decay
        o = jnp.einsum('bhid,bhdv->bhiv', qi * jnp.exp(gcum)[..., None], S) + jnp.einsum('bhij,bhjv->bhiv', attn, v_new)
        g_last = gcum[..., -1:]
        S = S * jnp.exp(g_last)[..., None] + jnp.einsum('bhjd,bhjv->bhdv', ki * jnp.exp(g_last - gcum)[..., None], v_new)
        return S, o

    S0 = jnp.zeros((B, H, dk, dv), F32)
    xs = (_to_chunks(q), _to_chunks(k), _to_chunks(v), _to_chunks(g[..., None])[..., 0], _to_chunks(beta[..., None])[..., 0])
    _, o = lax.scan(step, S0, xs)
    return _from_chunks(o)


def _rwkv7_scan(r, decay, k, v, kk, a):
    B, L, H, N = r.shape

    def step(S, inp):
        rt, wt, kt, vt, kkt, at = inp
        sa = jnp.einsum('bhvk,bhk->bhv', S, -kkt)
        S = S * wt[:, :, None, :] + sa[..., None] * (kkt * at)[:, :, None, :] + vt[..., None] * kt[:, :, None, :]
        return S, jnp.einsum('bhvk,bhk->bhv', S, rt)

    xs = tuple(jnp.moveaxis(t, 1, 0) for t in (r, decay, k, v, kk, a))
    _, y = lax.scan(step, jnp.zeros((B, H, N, N), F32), xs)
    return jnp.moveaxis(y, 0, 1)


def _retention_branch(q, k, v, gate, norm_gain):
    B, L, _ = q.shape
    pos = jnp.arange(L)
    q = _rope(_heads(q.astype(F32), RET_HEADS), pos) * RET_DK ** -0.5
    k = _rope(_heads(k.astype(F32), RET_HEADS), pos)
    v = _heads(v.astype(F32), RET_HEADS)
    gamma = 1.0 - 2.0 ** (-5.0 - jnp.arange(RET_HEADS, dtype=F32))
    o = _retention_chunked(q, k, v, gamma)
    o = _head_norm(o, norm_gain.reshape(RET_HEADS, RET_DV), center=True)
    return (o.reshape(B, L, MIX_W) * jax.nn.silu(gate.astype(F32))).astype(gate.dtype)


def _gla_branch(q, k, v, gate, lr, w2, bias, norm_gain):
    B, L, _ = q.shape
    q = _heads(q.astype(F32), GLA_HEADS) * GLA_DK ** -0.5
    k = _heads(k.astype(F32), GLA_HEADS)
    v = _heads(v.astype(F32), GLA_HEADS)
    log_a = _heads(jax.nn.log_sigmoid(lr.astype(F32) @ w2 + bias) / GLA_TAU, GLA_HEADS)
    o = _gla_chunked(q, k, v, log_a)
    o = _head_norm(o, norm_gain.reshape(GLA_HEADS, GLA_DV))
    return (o.reshape(B, L, MIX_W) * jax.nn.silu(gate.astype(F32))).astype(gate.dtype)


def _gdn_branch(q, k, v, z, a, b, conv_w, a_log, dt_bias, norm_gain):
    B, L, _ = q.shape
    qkv = jax.nn.silu(_causal_dwconv(jnp.concatenate([q, k, v], axis=-1).astype(F32), conv_w.astype(F32)))
    q, k, v = jnp.split(qkv, 3, axis=-1)
    q = _l2norm(_heads(q, GDN_HEADS)) * GDN_DK ** -0.5
    k = _l2norm(_heads(k, GDN_HEADS))
    v = _heads(v, GDN_HEADS)
    g = -jnp.exp(a_log.astype(F32)) * jax.nn.softplus(a.astype(F32) + dt_bias)
    beta = jax.nn.sigmoid(b.astype(F32))
    o = _gated_delta_chunked(q, k, v, g, beta)
    o = _head_norm(o, norm_gain) * jax.nn.silu(_heads(z.astype(F32), GDN_HEADS))
    return o.reshape(B, L, MIX_W).astype(z.dtype)


def _rwkv7_branch(src, mu, w0, w2, a0, a2, g2, k_k, k_a, r_k, ln_w, ln_b):
    B, L, _ = src.shape
    s = src.astype(F32)
    s = s + (_token_shift(s) - s) * mu.astype(F32)
    r, k, v, w_src, a_src, g_src = jnp.split(s, RWKV_SPLIT_IDX, axis=-1)
    decay = jnp.exp(-RWKV_DECAY_SCALE * jax.nn.sigmoid(w0 + jnp.tanh(w_src) @ w2))
    a = jax.nn.sigmoid(a0 + a_src @ a2)
    g = jax.nn.sigmoid(g_src) @ g2
    kk = _l2norm(_heads(k * k_k, RWKV_HEADS))
    k = k * (1.0 + (a - 1.0) * k_a)
    r, k, v, decay, a = (_heads(t, RWKV_HEADS) for t in (r, k, v, decay, a))
    y = _rwkv7_scan(r, decay, k, v, kk, a)
    y = _head_norm(y, ln_w.reshape(RWKV_HEADS, RWKV_HEAD), eps=RWKV_GN_EPS, center=True) + ln_b.reshape(RWKV_HEADS, RWKV_HEAD)
    y = y + jnp.sum(r * k * r_k, axis=-1, keepdims=True) * v
    return (y.reshape(B, L, MIX_W) * g).astype(src.dtype)


def _mixer_sublayer(x, pre_g, w_in, ret_norm, gla_w2, gla_b, gla_norm, gdn_conv, gdn_a_log, gdn_dt_bias,
                    gdn_norm, rwkv_mu, rwkv_w0, rwkv_w2, rwkv_a0, rwkv_a2, rwkv_g2, rwkv_kk, rwkv_ka, rwkv_rk,
                    rwkv_ln_w, rwkv_ln_b, w_branch, w_gate, w_out, post_g):
    B, L, D = x.shape
    h = _rmsnorm(x, pre_g)
    (qa, ka, va, ga, qb, kb, vb, gb, lb, qc, kc, vc, zc, ac, bc, src_d) = jnp.split(h @ w_in, SPLIT_IDX, axis=-1)
    o_a = _retention_branch(qa, ka, va, ga, ret_norm)
    o_b = _gla_branch(qb, kb, vb, gb, lb, gla_w2, gla_b, gla_norm)
    o_c = _gdn_branch(qc, kc, vc, zc, ac, bc, gdn_conv, gdn_a_log, gdn_dt_bias, gdn_norm)
    o_d = _rwkv7_branch(src_d, rwkv_mu, rwkv_w0, rwkv_w2, rwkv_a0, rwkv_a2, rwkv_g2, rwkv_kk, rwkv_ka, rwkv_rk,
                        rwkv_ln_w, rwkv_ln_b)
    o = jnp.stack([o_a, o_b, o_c, o_d], axis=2)
    y = jnp.einsum('blnc,ncd->blnd', o, w_branch)
    gate = jax.nn.sigmoid((h @ w_gate).reshape(B, L, N_BRANCH, D))
    merged = jnp.sum(gate * y, axis=2)
    return x + _rmsnorm(merged @ w_out, post_g)


def _ffn_sublayer(x, pre_g, w_up, conv_w, w_down, post_g):
    h = _rmsnorm(x, pre_g)
    u = _causal_dwconv(h @ w_up, conv_w)
    a, b = jnp.split(u, 2, axis=-1)
    return x + _rmsnorm((jax.nn.silu(a) * b) @ w_down, post_g)


def setup_inputs(seed: int = 0) -> dict:
    key = jax.random.key(seed)
    ks = iter(jax.random.split(key, 48))

    def nrm(shape, scale):
        return jax.random.normal(next(ks), shape, F32) * scale

    def unif(shape, lo, hi):
        return jax.random.uniform(next(ks), shape, F32, lo, hi)

    def gain(shape):
        return 1.0 + nrm(shape, 0.02)

    dt = jnp.exp(unif((DEPTH, GDN_HEADS), math.log(1e-3), math.log(1e-1)))
    return {
        'x': nrm((BATCH, SEQ, D_MODEL), 1.0),
        'meta': nrm((N_META, D_MODEL), 1.0),
        'pre_mix': gain((DEPTH, D_MODEL)),
        'w_in': nrm((DEPTH, D_MODEL, IN_W), D_MODEL ** -0.5),
        'ret_norm': gain((DEPTH, MIX_W)),
        'gla_w2': nrm((DEPTH, GLA_RANK, GLA_HEADS * GLA_DK), GLA_RANK ** -0.5),
        'gla_b': nrm((DEPTH, GLA_HEADS * GLA_DK), 0.01),
        'gla_norm': gain((DEPTH, MIX_W)),
        'gdn_conv': nrm((DEPTH, GDN_CONV, 3 * MIX_W), GDN_CONV ** -0.5),
        'gdn_a_log': jnp.log(unif((DEPTH, GDN_HEADS), 1.0, 16.0)),
        'gdn_dt_bias': dt + jnp.log(-jnp.expm1(-dt)),
        'gdn_norm': gain((DEPTH, GDN_DV)),
        'rwkv_mu': unif((DEPTH, RWKV_SRC_W), 0.0, 1.0),
        'rwkv_w0': nrm((DEPTH, MIX_W), 1.0),
        'rwkv_w2': nrm((DEPTH, RWKV_W_RANK, MIX_W), 0.1 * RWKV_W_RANK ** -0.5),
        'rwkv_a0': nrm((DEPTH, MIX_W), 0.1),
        'rwkv_a2': nrm((DEPTH, RWKV_A_RANK, MIX_W), 0.1 * RWKV_A_RANK ** -0.5),
        'rwkv_g2': nrm((DEPTH, RWKV_G_RANK, MIX_W), RWKV_G_RANK ** -0.5),
        'rwkv_kk': 0.85 + nrm((DEPTH, MIX_W), 0.02),
        'rwkv_ka': 1.0 + nrm((DEPTH, MIX_W), 0.02),
        'rwkv_rk': nrm((DEPTH, RWKV_HEADS, RWKV_HEAD), 0.1),
        'rwkv_ln_w': gain((DEPTH, MIX_W)),
        'rwkv_ln_b': nrm((DEPTH, MIX_W), 0.01),
        'w_branch': nrm((DEPTH, N_BRANCH, MIX_W, D_MODEL), MIX_W ** -0.5),
        'w_gate': nrm((DEPTH, D_MODEL, N_BRANCH * D_MODEL), D_MODEL ** -0.5),
        'w_out': nrm((DEPTH, D_MODEL, D_MODEL), D_MODEL ** -0.5),
        'post_mix': gain((DEPTH, D_MODEL)),
        'pre_ffn': gain((DEPTH, D_MODEL)),
        'w_up': nrm((DEPTH, D_MODEL, 2 * D_FF), D_MODEL ** -0.5),
        'ffn_conv': nrm((DEPTH, FFN_CONV, 2 * D_FF), FFN_CONV ** -0.5),
        'w_down': nrm((DEPTH, D_FF, D_MODEL), D_FF ** -0.5),
        'post_ffn': gain((DEPTH, D_MODEL)),
    }


def reference(x, meta, pre_mix, w_in, ret_norm, gla_w2, gla_b, gla_norm, gdn_conv, gdn_a_log, gdn_dt_bias,
              gdn_norm, rwkv_mu, rwkv_w0, rwkv_w2, rwkv_a0, rwkv_a2, rwkv_g2, rwkv_kk, rwkv_ka, rwkv_rk,
              rwkv_ln_w, rwkv_ln_b, w_branch, w_gate, w_out, post_mix, pre_ffn, w_up, ffn_conv, w_down, post_ffn):
    B = x.shape[0]
    h = jnp.concatenate([jnp.broadcast_to(meta.astype(x.dtype)[None], (B, N_META, x.shape[-1])), x], axis=1)
    for i in range(DEPTH):
        h = _mixer_sublayer(h, pre_mix[i], w_in[i], ret_norm[i], gla_w2[i], gla_b[i], gla_norm[i], gdn_conv[i],
                            gdn_a_log[i], gdn_dt_bias[i], gdn_norm[i], rwkv_mu[i], rwkv_w0[i], rwkv_w2[i],
                            rwkv_a0[i], rwkv_a2[i], rwkv_g2[i], rwkv_kk[i], rwkv_ka[i], rwkv_rk[i], rwkv_ln_w[i],
                            rwkv_ln_b[i], w_branch[i], w_gate[i], w_out[i], post_mix[i])
        h = _ffn_sublayer(h, pre_ffn[i], w_up[i], ffn_conv[i], w_down[i], post_ffn[i])
    return h[:, N_META:]
```

```python
import functools
import math

import numpy as np
import jax
import jax.numpy as jnp
from jax import lax
from jax.experimental import pallas as pl
from jax.experimental.pallas import tpu as pltpu

F32 = jnp.float32
BF16 = jnp.bfloat16

D_MODEL = 4096
N_META = 16
N_BRANCH = 4
MIX_W = D_MODEL // N_BRANCH
EPS = 1e-6
D_FF = 11008
FFN_CONV = 3

RET_HEADS, RET_D = 4, 256
ROPE_BASE = 10000.0
GLA_HEADS, GLA_DK, GLA_DV, GLA_RANK, GLA_TAU = 4, 128, 256, 16, 16.0
GDN_HEADS, GDN_D, GDN_CONV = 8, 128, 4
RWKV_HEADS, RWKV_N = 16, 64
RWKV_W_RANK, RWKV_A_RANK, RWKV_G_RANK = 64, 64, 160
RWKV_DECAY_SCALE = 0.606531
RWKV_GN_EPS = 64e-5

LANE = 128
CH = 64
SUB = 16
ROWS = 320
VMEM_LIMIT = 56 * 1024 * 1024

GLA_PW = 2 * GLA_HEADS * GLA_DK + 2 * GLA_HEADS * GLA_DV + LANE
GDN_PW = 4 * MIX_W + LANE
RWKV_LR = 3 * MIX_W
RWKV_PW = 3 * MIX_W + LANE + 2 * LANE


def _row_tile(m, pref):
    best = 16
    for t in range(16, min(m, pref) + 1, 16):
        if m % t == 0:
            best = t
    assert m % best == 0, (m, pref)
    return best


def _cparams(sem):
    return pltpu.CompilerParams(dimension_semantics=sem, vmem_limit_bytes=VMEM_LIMIT)


def _bdot(a, b):
    return jnp.dot(a.astype(BF16), b.astype(BF16), preferred_element_type=F32)


def _bdot_nt(a, b):
    return lax.dot_general(a.astype(BF16), b.astype(BF16), (((1,), (1,)), ((), ())),
                           preferred_element_type=F32)


def _bdot_tn(a, b):
    return lax.dot_general(a.astype(BF16), b.astype(BF16), (((0,), (0,)), ((), ())),
                           preferred_element_type=F32)


def _mask_dot(mask, x):
    hi = x.astype(BF16)
    lo = (x - hi.astype(F32)).astype(BF16)
    return (jnp.dot(mask, hi, preferred_element_type=F32)
            + jnp.dot(mask, lo, preferred_element_type=F32))


def _sigmoid(x):
    return 1.0 / (1.0 + jnp.exp(-x))


def _silu(x):
    return x * _sigmoid(x)


def _softplus(x):
    return jnp.maximum(x, 0.0) + jnp.log1p(jnp.exp(-jnp.abs(x)))


def _inv_unit_lower(a, same_blk, eye):
    d = a * same_blk
    l = a - d
    d2 = _bdot(d, d)
    d4 = _bdot(d2, d2)
    d8 = _bdot(d4, d4)
    p = eye - d
    p = p + _bdot(p, d2)
    p = p + _bdot(p, d4)
    td = p + _bdot(p, d8)
    x = _bdot(td, l)
    x2 = _bdot(x, x)
    q = eye - x
    q = q + _bdot(q, x2)
    return _bdot(q, td)


def _np_masks():
    i = np.arange(CH)[:, None]
    j = np.arange(CH)[None, :]
    return i, j


def _const_tri():
    i, j = _np_masks()
    return jnp.asarray((j <= i).astype(np.float32), BF16)


def _const_same_blk():
    i, j = _np_masks()
    return jnp.asarray((i // SUB == j // SUB).astype(np.float32), F32)


def _prenorm_kernel(x_ref, g_ref, o_ref):
    x = x_ref[...]
    y = x * lax.rsqrt(jnp.mean(x * x, axis=-1, keepdims=True) + EPS)
    o_ref[...] = (y * g_ref[...]).astype(o_ref.dtype)


def _prenorm(x, g):
    m, d = x.shape
    tm = _row_tile(m, 416)
    return pl.pallas_call(
        _prenorm_kernel,
        grid=(m // tm,),
        in_specs=[pl.BlockSpec((tm, d), lambda i: (i, 0)),
                  pl.BlockSpec((1, d), lambda i: (0, 0))],
        out_specs=pl.BlockSpec((tm, d), lambda i: (i, 0)),
        out_shape=jax.ShapeDtypeStruct((m, d), BF16),
        compiler_params=_cparams(("parallel",)),
        name="prenorm",
    )(x, g.reshape(1, d))


def _norm_residual_kernel(z_ref, x_ref, pg_ref, ng_ref, xo_ref, ho_ref):
    z = z_ref[...]
    y = z * lax.rsqrt(jnp.mean(z * z, axis=-1, keepdims=True) + EPS) * pg_ref[...]
    xn = x_ref[...] + y
    xo_ref[...] = xn
    h = xn * lax.rsqrt(jnp.mean(xn * xn, axis=-1, keepdims=True) + EPS) * ng_ref[...]
    ho_ref[...] = h.astype(ho_ref.dtype)


def _norm_residual(z, x, post_g, next_g):
    m, d = x.shape
    tm = _row_tile(m, 208)
    row = pl.BlockSpec((tm, d), lambda i: (i, 0))
    vec = pl.BlockSpec((1, d), lambda i: (0, 0))
    return pl.pallas_call(
        _norm_residual_kernel,
        grid=(m // tm,),
        in_specs=[row, row, vec, vec],
        out_specs=[row, row],
        out_shape=[jax.ShapeDtypeStruct((m, d), F32), jax.ShapeDtypeStruct((m, d), BF16)],
        compiler_params=_cparams(("parallel",)),
        name="norm_residual",
    )(z, x, post_g.reshape(1, d), next_g.reshape(1, d))


def _matmul_kernel(x_ref, w_ref, o_ref):
    o_ref[...] = jnp.dot(x_ref[...], w_ref[...], preferred_element_type=F32).astype(o_ref.dtype)


def _matmul(x, w, tm, tn, out_dtype, name):
    m, k = x.shape
    n = w.shape[1]
    tm = _row_tile(m, tm)
    assert n % tn == 0, (n, tn)
    return pl.pallas_call(
        _matmul_kernel,
        grid=(m // tm, n // tn),
        in_specs=[pl.BlockSpec((tm, k), lambda i, j: (i, 0)),
                  pl.BlockSpec((k, tn), lambda i, j: (0, j))],
        out_specs=pl.BlockSpec((tm, tn), lambda i, j: (i, j)),
        out_shape=jax.ShapeDtypeStruct((m, n), out_dtype),
        compiler_params=_cparams(("parallel", "parallel")),
        name=name,
    )(x, w)


def _gate_merge_kernel(h_ref, o_ref, wg_ref, wb_ref, out_ref, acc_ref):
    n = pl.program_id(2)
    gate = _sigmoid(jnp.dot(h_ref[...], wg_ref[...], preferred_element_type=F32))
    y = jnp.dot(o_ref[...], wb_ref[...], preferred_element_type=F32)

    @pl.when(n == 0)
    def _():
        acc_ref[...] = gate * y

    @pl.when(n > 0)
    def _():
        acc_ref[...] += gate * y

    @pl.when(n == N_BRANCH - 1)
    def _():
        out_ref[...] = acc_ref[...].astype(out_ref.dtype)


def _gate_merge(h, o4, w_gate, w_branch, tn=512):
    m, d = h.shape
    tm = _row_tile(m, 1040)
    nj = d // tn
    return pl.pallas_call(
        _gate_merge_kernel,
        grid=(m // tm, nj, N_BRANCH),
        in_specs=[pl.BlockSpec((tm, d), lambda i, j, n: (i, 0)),
                  pl.BlockSpec((None, tm, MIX_W), lambda i, j, n: (n, i, 0)),
                  pl.BlockSpec((d, tn), lambda i, j, n: (0, n * nj + j)),
                  pl.BlockSpec((None, MIX_W, tn), lambda i, j, n: (n, 0, j))],
        out_specs=pl.BlockSpec((tm, tn), lambda i, j, n: (i, j)),
        out_shape=jax.ShapeDtypeStruct((m, d), BF16),
        scratch_shapes=[pltpu.VMEM((tm, tn), F32)],
        compiler_params=_cparams(("parallel", "parallel", "arbitrary")),
        name="gate_merge",
    )(h, o4, w_gate, w_branch)


def _ffn_up_kernel(h_ref, halo_ref, wa_ref, wb_ref, ca_ref, cb_ref, o_ref, *, tiles_per_seq):
    i = pl.program_id(0)
    first = (i % tiles_per_seq) == 0
    h = h_ref[...]
    halo = halo_ref[...]
    tm = h.shape[0]
    row = lax.broadcasted_iota(jnp.int32, (tm, 1), 0)

    def conv(w_ref, c_ref):
        u = jnp.dot(h, w_ref[...], preferred_element_type=F32)
        uh = jnp.dot(halo, w_ref[...], preferred_element_type=F32)
        uh = jnp.where(first, 0.0, uh)
        p1, p2 = uh[15:16], uh[14:15]
        u1 = jnp.where(row == 0, p1, pltpu.roll(u, 1, 0))
        u2 = pltpu.roll(u, 2, 0)
        u2 = jnp.where(row == 0, p2, jnp.where(row == 1, p1, u2))
        c = c_ref[...]
        return c[2:3] * u + c[1:2] * u1 + c[0:1] * u2

    a = conv(wa_ref, ca_ref)
    b = conv(wb_ref, cb_ref)
    o_ref[...] = (_silu(a) * b).astype(o_ref.dtype)


def _ffn_up(h, w_up, conv_w, lp, tn=256):
    m, d = h.shape
    tm = _row_tile(lp, 1040)
    nj = D_FF // tn
    halo_blk = tm // 16
    kern = functools.partial(_ffn_up_kernel, tiles_per_seq=lp // tm)
    return pl.pallas_call(
        kern,
        grid=(m // tm, nj),
        in_specs=[pl.BlockSpec((tm, d), lambda i, j: (i, 0)),
                  pl.BlockSpec((16, d), lambda i, j: (jnp.maximum(i * halo_blk - 1, 0), 0)),
                  pl.BlockSpec((d, tn), lambda i, j: (0, j)),
                  pl.BlockSpec((d, tn), lambda i, j: (0, nj + j)),
                  pl.BlockSpec((FFN_CONV, tn), lambda i, j: (0, j)),
                  pl.BlockSpec((FFN_CONV, tn), lambda i, j: (0, nj + j))],
        out_specs=pl.BlockSpec((tm, tn), lambda i, j: (i, j)),
        out_shape=jax.ShapeDtypeStruct((m, D_FF), BF16),
        compiler_params=_cparams(("parallel", "parallel")),
        name="ffn_up",
    )(h, h, w_up, w_up, conv_w, conv_w)


def _retention_kernel(q_ref, k_ref, v_ref, g_ref, cos_ref, sin_ref, lg_ref, gain_ref, o_ref, s_ref):
    r = pl.program_id(2)

    @pl.when(r == 0)
    def _():
        s_ref[...] = jnp.zeros_like(s_ref)

    half = RET_D // 2
    cos, sin = cos_ref[...], sin_ref[...]

    def rope(t):
        t1, t2 = t[:, :half], t[:, half:]
        return jnp.concatenate([t1 * cos - t2 * sin, t1 * sin + t2 * cos], axis=1)

    q = rope(q_ref[...]) * RET_D ** -0.5
    k = rope(k_ref[...])
    v = v_ref[...]
    n = q.shape[0]
    lg = lg_ref[...][:, :1]
    ri = lax.broadcasted_iota(jnp.int32, (n, n), 0)
    ci = lax.broadcasted_iota(jnp.int32, (n, n), 1)
    rel = (ri - ci).astype(F32)
    decay = jnp.where(rel >= 0, jnp.exp(lg * jnp.maximum(rel, 0.0)), 0.0)
    idx = lax.broadcasted_iota(jnp.int32, (n, 1), 0).astype(F32)
    q_decay = jnp.exp(lg * (idx + 1.0))
    k_decay = jnp.exp(lg * (n - 1.0 - idx))
    s = s_ref[...]
    scores = _bdot_nt(q, k) * decay
    o = _bdot(scores, v) + _bdot(q, s) * q_decay
    s_ref[...] = s * jnp.exp(lg * float(n)) + _bdot_tn(k * k_decay, v)
    o = o - jnp.mean(o, axis=-1, keepdims=True)
    o = o * lax.rsqrt(jnp.mean(o * o, axis=-1, keepdims=True) + EPS) * gain_ref[...]
    o_ref[...] = (o * _silu(g_ref[...])).astype(o_ref.dtype)


def _retention(proj, cos, sin, norm_gain, batch, lp):
    m = proj.shape[0]
    nr = lp // ROWS
    gamma = 1.0 - 2.0 ** (-5.0 - np.arange(RET_HEADS, dtype=np.float64))
    lg = jnp.asarray(np.broadcast_to(np.log(gamma)[:, None, None], (RET_HEADS, 1, LANE)), F32)

    def part(p):
        return pl.BlockSpec((ROWS, RET_D), lambda b, h, r: (b * nr + r, p * RET_HEADS + h))

    tab = pl.BlockSpec((ROWS, RET_D // 2), lambda b, h, r: (r, 0))
    return pl.pallas_call(
        _retention_kernel,
        grid=(batch, RET_HEADS, nr),
        in_specs=[part(0), part(1), part(2), part(3), tab, tab,
                  pl.BlockSpec((None, 1, LANE), lambda b, h, r: (h, 0, 0)),
                  pl.BlockSpec((1, RET_D), lambda b, h, r: (0, h))],
        out_specs=pl.BlockSpec((ROWS, RET_D), lambda b, h, r: (b * nr + r, h)),
        out_shape=jax.ShapeDtypeStruct((m, MIX_W), BF16),
        scratch_shapes=[pltpu.VMEM((RET_D, RET_D), F32)],
        compiler_params=_cparams(("parallel", "parallel", "arbitrary")),
        name="retention",
    )(proj, proj, proj, proj, cos, sin, lg, norm_gain.reshape(1, MIX_W))


GLA_LEVELS = 6


def _gla_consts():
    i, j = _np_masks()
    mats = [(j <= i), (j > i)]
    pair = []
    for lvl in range(GLA_LEVELS):
        s = 32 >> lvl
        blk = (i // (2 * s)) * (2 * s)
        mid = blk + s - 1
        second = (i - blk) >= s
        mats.append(second & (j > mid) & (j <= i))
        mats.append((~second) & (j > i) & (j <= mid))
        jb = (j // (2 * s)) * (2 * s)
        pair.append(second & (jb == blk) & ((j - jb) < s))
    pair.append(i == j)
    em = np.concatenate(mats, axis=0).astype(np.float32)
    pm = np.stack(pair, axis=0).astype(np.float32)
    return jnp.asarray(em, BF16), jnp.asarray(pm, F32)


def _gla_kernel(q_ref, k_ref, v_ref, g_ref, lr_ref, w2_ref, bias_ref, gain_ref, em_ref, pm_ref,
                o_ref, s_ref):
    r = pl.program_id(2)

    @pl.when(r == 0)
    def _():
        s_ref[...] = jnp.zeros_like(s_ref)

    em = em_ref[...]
    w2 = w2_ref[...]
    bias = bias_ref[...]
    gain = gain_ref[...]
    for c in range(ROWS // CH):
        rows = pl.ds(c * CH, CH)
        q = q_ref[rows, :] * GLA_DK ** -0.5
        k = k_ref[rows, :]
        v = v_ref[rows, :]
        x = _bdot(lr_ref[rows, :], w2) + bias
        la = (jnp.minimum(x, 0.0) - jnp.log1p(jnp.exp(-jnp.abs(x)))) / GLA_TAU
        ex = jnp.exp(_mask_dot(em, la))
        e_b = ex[0:CH]
        e_rest = ex[CH:2 * CH]
        scores = _bdot_nt(q, k) * pm_ref[GLA_LEVELS]
        for lvl in range(GLA_LEVELS):
            eq = ex[(2 + 2 * lvl) * CH:(3 + 2 * lvl) * CH]
            ek = ex[(3 + 2 * lvl) * CH:(4 + 2 * lvl) * CH]
            scores = scores + _bdot_nt(q * eq, k * ek) * pm_ref[lvl]
        st = s_ref[...]
        o = _bdot(scores, v) + _bdot_nt(q * e_b, st)
        s_ref[...] = st * e_b[CH - 1:CH, :] + _bdot_tn(v, k * e_rest)
        o = o * lax.rsqrt(jnp.mean(o * o, axis=-1, keepdims=True) + EPS) * gain
        o_ref[rows, :] = (o * _silu(g_ref[rows, :])).astype(o_ref.dtype)


def _gla(proj, w2p, bias, norm_gain, batch, lp):
    m = proj.shape[0]
    nr = lp // ROWS
    em, pm = _gla_consts()
    kb = GLA_HEADS
    vb = 2 * GLA_HEADS * GLA_DK // GLA_DV
    gb = vb + GLA_HEADS
    lrb = (2 * GLA_HEADS * GLA_DK + 2 * GLA_HEADS * GLA_DV) // LANE
    return pl.pallas_call(
        _gla_kernel,
        grid=(batch, GLA_HEADS, nr),
        in_specs=[pl.BlockSpec((ROWS, GLA_DK), lambda b, h, r: (b * nr + r, h)),
                  pl.BlockSpec((ROWS, GLA_DK), lambda b, h, r: (b * nr + r, kb + h)),
                  pl.BlockSpec((ROWS, GLA_DV), lambda b, h, r: (b * nr + r, vb + h)),
                  pl.BlockSpec((ROWS, GLA_DV), lambda b, h, r: (b * nr + r, gb + h)),
                  pl.BlockSpec((ROWS, LANE), lambda b, h, r: (b * nr + r, lrb)),
                  pl.BlockSpec((LANE, GLA_DK), lambda b, h, r: (0, h)),
                  pl.BlockSpec((1, GLA_DK), lambda b, h, r: (0, h)),
                  pl.BlockSpec((1, GLA_DV), lambda b, h, r: (0, h)),
                  pl.BlockSpec(em.shape, lambda b, h, r: (0, 0)),
                  pl.BlockSpec(pm.shape, lambda b, h, r: (0, 0, 0))],
        out_specs=pl.BlockSpec((ROWS, GLA_DV), lambda b, h, r: (b * nr + r, h)),
        out_shape=jax.ShapeDtypeStruct((m, MIX_W), BF16),
        scratch_shapes=[pltpu.VMEM((GLA_DV, GLA_DK), F32)],
        compiler_params=_cparams(("parallel", "parallel", "arbitrary")),
        name="gla",
    )(proj, proj, proj, proj, proj, w2p, bias.reshape(1, -1), norm_gain.reshape(1, MIX_W), em, pm)


def _gdn_kernel(q_ref, k_ref, v_ref, z_ref, ab_ref, cq_ref, ck_ref, cv_ref, alog_ref, dtb_ref,
                gain_ref, tri_ref, blk_ref, o_ref, s_ref, carry_ref):
    h = pl.program_id(1)
    r = pl.program_id(2)

    @pl.when(r == 0)
    def _():
        s_ref[...] = jnp.zeros_like(s_ref)
        carry_ref[...] = jnp.zeros_like(carry_ref)

    def conv_silu(x_ref, c_ref, slot):
        x = x_ref[...]
        ext = jnp.concatenate([carry_ref[slot], x], axis=0)
        carry_ref[slot] = x[ROWS - 8:, :]
        c = c_ref[...]
        y = c[3:4] * x
        for t in range(1, GDN_CONV):
            y = y + c[3 - t:4 - t] * ext[8 - t:8 - t + ROWS, :]
        return _silu(y)

    def l2norm(t):
        return t * lax.rsqrt(jnp.sum(t * t, axis=-1, keepdims=True) + EPS)

    q_all = l2norm(conv_silu(q_ref, cq_ref, 0)) * GDN_D ** -0.5
    k_all = l2norm(conv_silu(k_ref, ck_ref, 1))
    v_all = conv_silu(v_ref, cv_ref, 2)

    lane = lax.broadcasted_iota(jnp.int32, (1, LANE), 1)
    pick = lambda t, idx: jnp.sum(jnp.where(lane == idx, t, 0.0), axis=-1, keepdims=True)
    ab = ab_ref[...]
    a_src = pick(ab, h)
    b_src = pick(ab, GDN_HEADS + h)
    a_log = pick(alog_ref[...], h)
    dt_bias = pick(dtb_ref[...], h)
    g_all = -jnp.exp(a_log) * _softplus(a_src + dt_bias)
    beta_all = _sigmoid(b_src)

    tri = tri_ref[...]
    same_blk = blk_ref[...]
    ri = lax.broadcasted_iota(jnp.int32, (CH, CH), 0)
    ci = lax.broadcasted_iota(jnp.int32, (CH, CH), 1)
    eye = (ri == ci).astype(F32)
    gain = gain_ref[...]
    for c in range(ROWS // CH):
        sl = slice(c * CH, (c + 1) * CH)
        q, k, v = q_all[sl], k_all[sl], v_all[sl]
        beta = beta_all[sl]
        gcum = _mask_dot(tri, jnp.broadcast_to(g_all[sl], (CH, LANE)))
        gcol = gcum[:, :1]
        grow = jnp.transpose(gcum)[:1, :CH]
        decay = jnp.where(ri >= ci, jnp.exp(jnp.minimum(gcol - grow, 0.0)), 0.0)
        kb = k * beta
        a_mat = jnp.where(ri > ci, _bdot_nt(kb, k) * decay, 0.0)
        t_inv = _inv_unit_lower(a_mat, same_blk, eye)
        e_g = jnp.exp(gcol)
        sol = _bdot(t_inv, jnp.concatenate([v * beta, kb * e_g], axis=1))
        u, w = sol[:, :GDN_D], sol[:, GDN_D:]
        s = s_ref[...]
        v_new = u - _bdot(w, s)
        attn = _bdot_nt(q, k) * decay
        o = _bdot(q * e_g, s) + _bdot(attn, v_new)
        g_last = gcum[CH - 1:CH, :1]
        s_ref[...] = s * jnp.exp(g_last) + _bdot_tn(k * jnp.exp(g_last - gcol), v_new)
        o = o * lax.rsqrt(jnp.mean(o * o, axis=-1, keepdims=True) + EPS) * gain
        o_ref[sl, :] = (o * _silu(z_ref[sl, :])).astype(o_ref.dtype)


def _gdn(proj, conv_w, a_log, dt_bias, norm_gain, batch, lp):
    m = proj.shape[0]
    nr = lp // ROWS
    nh = GDN_HEADS
    pad = lambda t: jnp.pad(t.reshape(1, nh), ((0, 0), (0, LANE - nh)))

    def part(p):
        return pl.BlockSpec((ROWS, GDN_D), lambda b, h, r: (b * nr + r, p * nh + h))

    def cpart(p):
        return pl.BlockSpec((GDN_CONV, GDN_D), lambda b, h, r: (0, p * nh + h))

    const = lambda shape: pl.BlockSpec(shape, lambda b, h, r: (0,) * len(shape))
    return pl.pallas_call(
        _gdn_kernel,
        grid=(batch, nh, nr),
        in_specs=[part(0), part(1), part(2), part(3),
                  pl.BlockSpec((ROWS, LANE), lambda b, h, r: (b * nr + r, 4 * nh)),
                  cpart(0), cpart(1), cpart(2),
                  const((1, LANE)), const((1, LANE)), const((1, GDN_D)),
                  const((CH, CH)), const((CH, CH))],
        out_specs=pl.BlockSpec((ROWS, GDN_D), lambda b, h, r: (b * nr + r, h)),
        out_shape=jax.ShapeDtypeStruct((m, MIX_W), BF16),
        scratch_shapes=[pltpu.VMEM((GDN_D, GDN_D), F32), pltpu.VMEM((3, 8, GDN_D), F32)],
        compiler_params=_cparams(("parallel", "parallel", "arbitrary")),
        name="gdn",
    )(proj, proj, proj, proj, proj, conv_w, conv_w, conv_w, pad(a_log), pad(dt_bias),
      norm_gain.reshape(1, GDN_D), _const_tri(), _const_same_blk())


def _rwkv_prep_kernel(src_ref, mu_ref, w0_ref, a0_ref, kk_ref, ka_ref, w2_ref, a2_ref, g2_ref,
                      seg_ref, segt_ref, r_ref, lw_ref, k_ref, v_ref, na_ref, bb_ref, g_ref, carry_ref):
    rb = pl.program_id(1)

    @pl.when(rb == 0)
    def _():
        carry_ref[...] = jnp.zeros_like(carry_ref)

    x = src_ref[...]
    ext = jnp.concatenate([carry_ref[...], x], axis=0)
    carry_ref[...] = x[ROWS - 8:, :]
    prev = ext[7:7 + ROWS, :]
    s = x + (prev - x) * mu_ref[...]
    r = s[:, :MIX_W]
    k = s[:, MIX_W:2 * MIX_W]
    v = s[:, 2 * MIX_W:3 * MIX_W]
    wa_src = s[:, RWKV_LR:RWKV_LR + LANE]
    g_src = s[:, RWKV_LR + LANE:]
    lw_ref[...] = -RWKV_DECAY_SCALE * _sigmoid(w0_ref[...] + _bdot(jnp.tanh(wa_src), w2_ref[...]))
    a = _sigmoid(a0_ref[...] + _bdot(wa_src, a2_ref[...]))
    g_ref[...] = _bdot(_sigmoid(g_src), g2_ref[...])
    kk = k * kk_ref[...]
    sq = kk * kk
    sq_hi = sq.astype(BF16)
    sq_lo = (sq - sq_hi.astype(F32)).astype(BF16)
    seg = seg_ref[...]
    hsum = (jnp.dot(sq_hi, seg, preferred_element_type=F32)
            + jnp.dot(sq_lo, seg, preferred_element_type=F32))
    inv = lax.rsqrt(hsum + EPS)
    inv_hi = inv.astype(BF16)
    inv_lo = (inv - inv_hi.astype(F32)).astype(BF16)
    segt = segt_ref[...]
    inv_full = (jnp.dot(inv_hi, segt, preferred_element_type=F32)
                + jnp.dot(inv_lo, segt, preferred_element_type=F32))
    kk = kk * inv_full
    r_ref[...] = r
    k_ref[...] = k * (1.0 + (a - 1.0) * ka_ref[...])
    v_ref[...] = v
    na_ref[...] = -kk
    bb_ref[...] = kk * a


def _rwkv_scan_kernel(r_ref, lw_ref, k_ref, v_ref, na_ref, bb_ref, g_ref, rk_ref, lnw_ref, lnb_ref,
                      cm_ref, blk_ref, o_ref, h_ref):
    rb = pl.program_id(2)

    @pl.when(rb == 0)
    def _():
        h_ref[...] = jnp.zeros_like(h_ref)

    n = RWKV_N
    ri = lax.broadcasted_iota(jnp.int32, (CH, CH), 0)
    ci = lax.broadcasted_iota(jnp.int32, (CH, CH), 1)
    eye = (ri == ci).astype(F32)
    strict = ri > ci
    incl = ri >= ci
    same_blk = blk_ref[...]
    cm = cm_ref[...]
    rk, lnw, lnb = rk_ref[...], lnw_ref[...], lnb_ref[...]
    for c in range(ROWS // CH):
        sl = pl.ds(c * CH, CH)
        r, lw, k, v = r_ref[sl, :], lw_ref[sl, :], k_ref[sl, :], v_ref[sl, :]
        na, bb, g = na_ref[sl, :], bb_ref[sl, :], g_ref[sl, :]
        cums = _mask_dot(cm, lw)
        cum, rest = cums[:CH], cums[CH:]
        e_cum = jnp.exp(cum)
        e_neg = jnp.exp(-cum)
        e_rest = jnp.exp(rest)
        rt = r * e_cum
        at = na * jnp.exp(cum - lw)
        bt = bb * e_neg
        kt = k * e_neg
        bh = bb * e_rest
        kh = k * e_rest
        e_last = e_cum[CH - 1:CH, :]
        outs = []
        for p in range(LANE // n):
            hs = slice(p * n, (p + 1) * n)
            pmat = _bdot_nt(jnp.concatenate([at[:, hs], rt[:, hs]], axis=0),
                            jnp.concatenate([bt[:, hs], kt[:, hs]], axis=0))
            a_ab = jnp.where(strict, pmat[:CH, :CH], 0.0)
            a_ak = jnp.where(strict, pmat[:CH, CH:], 0.0)
            a_rb = jnp.where(incl, pmat[CH:, :CH], 0.0)
            a_rk = jnp.where(incl, pmat[CH:, CH:], 0.0)
            t_inv = _inv_unit_lower(-a_ab, same_blk, eye)
            vh = v[:, hs]
            sol = _bdot(t_inv, jnp.concatenate([at[:, hs], _bdot(a_ak, vh)], axis=1))
            ht = h_ref[p]
            u = _bdot_nt(sol[:, :n], ht) + sol[:, n:]
            y = _bdot_nt(rt[:, hs], ht) + _bdot(a_rb, u) + _bdot(a_rk, vh)
            h_ref[p] = ht * e_last[:, hs] + _bdot_tn(u, bh[:, hs]) + _bdot_tn(vh, kh[:, hs])
            y = y - jnp.mean(y, axis=-1, keepdims=True)
            y = y * lax.rsqrt(jnp.mean(y * y, axis=-1, keepdims=True) + RWKV_GN_EPS)
            y = y * lnw[:, hs] + lnb[:, hs]
            bonus = jnp.sum(r[:, hs] * k[:, hs] * rk[:, hs], axis=-1, keepdims=True)
            outs.append((y + bonus * vh) * g[:, hs])
        o_ref[sl, :] = jnp.concatenate(outs, axis=1).astype(o_ref.dtype)


def _rwkv(proj, mu_p, w0, w2p, a0, a2p, g2p, kk, ka, rk, ln_w, ln_b, batch, lp):
    m = proj.shape[0]
    nr = lp // ROWS
    c = np.arange(MIX_W)[:, None] // RWKV_N == np.arange(LANE)[None, :]
    seg = jnp.asarray(c.astype(np.float32), BF16)
    segt = jnp.asarray(c.T.astype(np.float32), BF16)
    vec = lambda t: t.reshape(1, -1)
    const2 = lambda shape: pl.BlockSpec(shape, lambda b, r: (0, 0))
    row_w = pl.BlockSpec((ROWS, MIX_W), lambda b, r: (b * nr + r, 0))
    wide = jax.ShapeDtypeStruct((m, MIX_W), F32)
    r_, lw, k_, v_, na, bb, g = pl.pallas_call(
        _rwkv_prep_kernel,
        grid=(batch, nr),
        in_specs=[pl.BlockSpec((ROWS, RWKV_PW), lambda b, r: (b * nr + r, 0)),
                  const2((1, RWKV_PW)), const2((1, MIX_W)), const2((1, MIX_W)), const2((1, MIX_W)),
                  const2((1, MIX_W)), const2((LANE, MIX_W)), const2((LANE, MIX_W)),
                  const2((2 * LANE, MIX_W)), const2((MIX_W, LANE)), const2((LANE, MIX_W))],
        out_specs=[row_w] * 7,
        out_shape=[wide] * 7,
        scratch_shapes=[pltpu.VMEM((8, RWKV_PW), F32)],
        compiler_params=_cparams(("parallel", "arbitrary")),
        name="rwkv_prep",
    )(proj, vec(mu_p), vec(w0), vec(a0), vec(kk), vec(ka), w2p, a2p, g2p, seg, segt)

    i, j = _np_masks()
    cm = jnp.asarray(np.concatenate([(j <= i), (j > i)], axis=0).astype(np.float32), BF16)
    pair = pl.BlockSpec((ROWS, LANE), lambda b, h, r: (b * nr + r, h))
    pvec = pl.BlockSpec((1, LANE), lambda b, h, r: (0, h))
    return pl.pallas_call(
        _rwkv_scan_kernel,
        grid=(batch, MIX_W // LANE, nr),
        in_specs=[pair] * 7 + [pvec] * 3
                 + [pl.BlockSpec((2 * CH, CH), lambda b, h, r: (0, 0)),
                    pl.BlockSpec((CH, CH), lambda b, h, r: (0, 0))],
        out_specs=pair,
        out_shape=jax.ShapeDtypeStruct((m, MIX_W), BF16),
        scratch_shapes=[pltpu.VMEM((LANE // RWKV_N, RWKV_N, RWKV_N), F32)],
        compiler_params=_cparams(("parallel", "parallel", "arbitrary")),
        name="rwkv_scan",
    )(r_, lw, k_, v_, na, bb, g, vec(rk), vec(ln_w), vec(ln_b), cm, _const_same_blk())


def _pad_cols(w, width):
    return jnp.pad(w, ((0, 0), (0, width - w.shape[1])))


def _pad_rows(w, rows, offset=0):
    return jnp.pad(w, ((offset, rows - offset - w.shape[0]), (0, 0)))


def _split_w_in(w_in):
    ret_w = 4 * MIX_W
    gla_w = 2 * GLA_HEADS * GLA_DK + 2 * GLA_HEADS * GLA_DV + GLA_RANK
    gdn_w = 4 * MIX_W + 2 * GDN_HEADS
    o1, o2, o3 = ret_w, ret_w + gla_w, ret_w + gla_w + gdn_w
    w_ret = w_in[:, :o1]
    w_gla = _pad_cols(w_in[:, o1:o2], GLA_PW)
    w_gdn = _pad_cols(w_in[:, o2:o3], GDN_PW)
    w_rwkv = _pad_cols(w_in[:, o3:], RWKV_PW)
    return tuple(t.astype(BF16) for t in (w_ret, w_gla, w_gdn, w_rwkv))


def _mixer_sublayer(x, h, lyr, cos, sin, batch, lp, next_g):
    w_ret, w_gla, w_gdn, w_rwkv = _split_w_in(lyr["w_in"])
    p_ret = _matmul(h, w_ret, 1040, 1024, F32, "proj_ret")
    p_gla = _matmul(h, w_gla, 1040, 640, F32, "proj_gla")
    p_gdn = _matmul(h, w_gdn, 1040, 384, F32, "proj_gdn")
    p_rwkv = _matmul(h, w_rwkv, 1040, 1152, F32, "proj_rwkv")

    o_a = _retention(p_ret, cos, sin, lyr["ret_norm"], batch, lp)
    o_b = _gla(p_gla, _pad_rows(lyr["gla_w2"], LANE).astype(BF16), lyr["gla_b"], lyr["gla_norm"],
               batch, lp)
    o_c = _gdn(p_gdn, lyr["gdn_conv"], lyr["gdn_a_log"], lyr["gdn_dt_bias"], lyr["gdn_norm"], batch, lp)
    mu_p = jnp.pad(lyr["rwkv_mu"], (0, RWKV_PW - lyr["rwkv_mu"].shape[0]))
    w2p = _pad_rows(lyr["rwkv_w2"], LANE).astype(BF16)
    a2p = _pad_rows(lyr["rwkv_a2"], LANE, RWKV_W_RANK).astype(BF16)
    g2p = _pad_rows(lyr["rwkv_g2"], 2 * LANE).astype(BF16)
    o_d = _rwkv(p_rwkv, mu_p, lyr["rwkv_w0"], w2p, lyr["rwkv_a0"], a2p, g2p, lyr["rwkv_kk"],
                lyr["rwkv_ka"], lyr["rwkv_rk"], lyr["rwkv_ln_w"], lyr["rwkv_ln_b"], batch, lp)

    o4 = jnp.stack([o_a, o_b, o_c, o_d], axis=0)
    merged = _gate_merge(h, o4, lyr["w_gate"].astype(BF16), lyr["w_branch"].astype(BF16))
    z = _matmul(merged, lyr["w_out"].astype(BF16), 1040, 1024, F32, "w_out")
    return _norm_residual(z, x, lyr["post_mix"], next_g)


def _ffn_sublayer(x, h, lyr, lp, next_g):
    act = _ffn_up(h, lyr["w_up"].astype(BF16), lyr["ffn_conv"], lp)
    z = _matmul(act, lyr["w_down"].astype(BF16), 520, 512, F32, "ffn_down")
    return _norm_residual(z, x, lyr["post_ffn"], next_g)


_LAYER_KEYS = ("pre_mix", "w_in", "ret_norm", "gla_w2", "gla_b", "gla_norm", "gdn_conv", "gdn_a_log",
               "gdn_dt_bias", "gdn_norm", "rwkv_mu", "rwkv_w0", "rwkv_w2", "rwkv_a0", "rwkv_a2", "rwkv_g2",
               "rwkv_kk", "rwkv_ka", "rwkv_rk", "rwkv_ln_w", "rwkv_ln_b", "w_branch", "w_gate", "w_out",
               "post_mix", "pre_ffn", "w_up", "ffn_conv", "w_down", "post_ffn")


def _trunk(x, meta, params):
    batch, seq, d = x.shape
    depth = params["pre_mix"].shape[0]
    l = N_META + seq
    lp = -(-l // ROWS) * ROWS
    hcat = jnp.concatenate([jnp.broadcast_to(meta.astype(x.dtype)[None], (batch, N_META, d)), x,
                            jnp.zeros((batch, lp - l, d), x.dtype)], axis=1)
    xr = hcat.reshape(batch * lp, d)

    half = RET_D // 2
    inv_freq = ROPE_BASE ** (-jnp.arange(half, dtype=F32) / half)
    ang = jnp.arange(lp, dtype=F32)[:, None] * inv_freq[None, :]
    cos, sin = jnp.cos(ang), jnp.sin(ang)

    layers = [{k: params[k][i] for k in _LAYER_KEYS} for i in range(depth)]
    h = _prenorm(xr, layers[0]["pre_mix"])
    for i, lyr in enumerate(layers):
        xr, h = _mixer_sublayer(xr, h, lyr, cos, sin, batch, lp, lyr["pre_ffn"])
        next_g = layers[i + 1]["pre_mix"] if i + 1 < depth else lyr["pre_ffn"]
        xr, h = _ffn_sublayer(xr, h, lyr, lp, next_g)
    return xr.reshape(batch, lp, d)[:, N_META:l]


def kernel(x, meta, pre_mix, w_in, ret_norm, gla_w2, gla_b, gla_norm, gdn_conv, gdn_a_log, gdn_dt_bias,
           gdn_norm, rwkv_mu, rwkv_w0, rwkv_w2, rwkv_a0, rwkv_a2, rwkv_g2, rwkv_kk, rwkv_ka, rwkv_rk,
           rwkv_ln_w, rwkv_ln_b, w_branch, w_gate, w_out, post_mix, pre_ffn, w_up, ffn_conv, w_down, post_ffn):
    params = dict(zip(_LAYER_KEYS, (pre_mix, w_in, ret_norm, gla_w2, gla_b, gla_norm, gdn_conv, gdn_a_log,
                                    gdn_dt_bias, gdn_norm, rwkv_mu, rwkv_w0, rwkv_w2, rwkv_a0, rwkv_a2,
                                    rwkv_g2, rwkv_kk, rwkv_ka, rwkv_rk, rwkv_ln_w, rwkv_ln_b, w_branch,
                                    w_gate, w_out, post_mix, pre_ffn, w_up, ffn_conv, w_down, post_ffn)))
    return _trunk(x, meta, params)
```

```python
import functools
import math

import numpy as np
import jax
import jax.numpy as jnp
from jax import lax
from jax.experimental import pallas as pl
from jax.experimental.pallas import tpu as pltpu

F32 = jnp.float32
BF16 = jnp.bfloat16

D_MODEL = 4096
N_META = 16
N_BRANCH = 4
MIX_W = D_MODEL // N_BRANCH
EPS = 1e-6
D_FF = 11008
FFN_CONV = 3

RET_HEADS, RET_D = 4, 256
ROPE_BASE = 10000.0
GLA_HEADS, GLA_DK, GLA_DV, GLA_RANK, GLA_TAU = 4, 128, 256, 16, 16.0
GDN_HEADS, GDN_D, GDN_CONV = 8, 128, 4
RWKV_HEADS, RWKV_N = 16, 64
RWKV_W_RANK, RWKV_A_RANK, RWKV_G_RANK = 64, 64, 160
RWKV_DECAY_SCALE = 0.606531
RWKV_GN_EPS = 64e-5

LANE = 128
CH = 64
SUB = 16
ROWS = 320
VMEM_LIMIT = 56 * 1024 * 1024

RWKV_LR = 3 * MIX_W
RWKV_PW = 3 * MIX_W + LANE + 2 * LANE
COL_RET = 0
COL_GDN = COL_RET + 4 * MIX_W
COL_RWKV = COL_GDN + 4 * MIX_W
COL_GLA_VG = COL_RWKV + 3 * MIX_W
COL_GLA_QK = COL_GLA_VG + 2 * MIX_W
COL_GDN_AB = COL_GLA_QK + 2 * GLA_HEADS * GLA_DK
COL_RWKV_WA = COL_GDN_AB + LANE
COL_RWKV_G = COL_RWKV_WA + LANE
COL_GLA_LR = COL_RWKV_G + 2 * LANE
PROJ_W = COL_GLA_LR + LANE


def _row_tile(m, pref):
    best = 16
    for t in range(16, min(m, pref) + 1, 16):
        if m % t == 0:
            best = t
    assert m % best == 0, (m, pref)
    return best


def _cparams(sem):
    return pltpu.CompilerParams(dimension_semantics=sem, vmem_limit_bytes=VMEM_LIMIT)


def _bdot(a, b):
    return jnp.dot(a.astype(BF16), b.astype(BF16), preferred_element_type=F32)


def _bdot_nt(a, b):
    return lax.dot_general(a.astype(BF16), b.astype(BF16), (((1,), (1,)), ((), ())),
                           preferred_element_type=F32)


def _bdot_tn(a, b):
    return lax.dot_general(a.astype(BF16), b.astype(BF16), (((0,), (0,)), ((), ())),
                           preferred_element_type=F32)


def _mask_dot(mask, x):
    hi = x.astype(BF16)
    lo = (x - hi.astype(F32)).astype(BF16)
    return (jnp.dot(mask, hi, preferred_element_type=F32)
            + jnp.dot(mask, lo, preferred_element_type=F32))


def _sigmoid(x):
    return 1.0 / (1.0 + jnp.exp(-x))


def _silu(x):
    return x * _sigmoid(x)


def _softplus(x):
    return jnp.maximum(x, 0.0) + jnp.log1p(jnp.exp(-jnp.abs(x)))


def _inv_unit_lower(mats, same_blk, eye):
    each = lambda f, *ls: [f(*t) for t in zip(*ls)]
    d = [a * same_blk for a in mats]
    l = each(lambda a, dd: a - dd, mats, d)
    d2 = each(_bdot, d, d)
    d4 = each(_bdot, d2, d2)
    d8 = each(_bdot, d4, d4)
    p = [eye - dd for dd in d]
    p = each(lambda pp, m: pp + _bdot(pp, m), p, d2)
    p = each(lambda pp, m: pp + _bdot(pp, m), p, d4)
    td = each(lambda pp, m: pp + _bdot(pp, m), p, d8)
    x = each(_bdot, td, l)
    x2 = each(_bdot, x, x)
    q = [eye - xx for xx in x]
    q = each(lambda qq, m: qq + _bdot(qq, m), q, x2)
    return each(_bdot, q, td)


def _np_masks():
    i = np.arange(CH)[:, None]
    j = np.arange(CH)[None, :]
    return i, j


def _const_tri():
    i, j = _np_masks()
    return jnp.asarray((j <= i).astype(np.float32), BF16)


def _const_same_blk():
    i, j = _np_masks()
    return jnp.asarray((i // SUB == j // SUB).astype(np.float32), F32)


def _prenorm_kernel(x_ref, g_ref, o_ref):
    x = x_ref[...]
    y = x * lax.rsqrt(jnp.mean(x * x, axis=-1, keepdims=True) + EPS)
    o_ref[...] = (y * g_ref[...]).astype(o_ref.dtype)


def _prenorm(x, g):
    m, d = x.shape
    tm = _row_tile(m, 416)
    return pl.pallas_call(
        _prenorm_kernel,
        grid=(m // tm,),
        in_specs=[pl.BlockSpec((tm, d), lambda i: (i, 0)),
                  pl.BlockSpec((1, d), lambda i: (0, 0))],
        out_specs=pl.BlockSpec((tm, d), lambda i: (i, 0)),
        out_shape=jax.ShapeDtypeStruct((m, d), BF16),
        compiler_params=_cparams(("parallel",)),
        name="prenorm",
    )(x, g.reshape(1, d))


def _norm_residual_kernel(z_ref, x_ref, pg_ref, ng_ref, xo_ref, ho_ref):
    z = z_ref[...]
    y = z * lax.rsqrt(jnp.mean(z * z, axis=-1, keepdims=True) + EPS) * pg_ref[...]
    xn = x_ref[...] + y
    xo_ref[...] = xn
    h = xn * lax.rsqrt(jnp.mean(xn * xn, axis=-1, keepdims=True) + EPS) * ng_ref[...]
    ho_ref[...] = h.astype(ho_ref.dtype)


def _norm_residual(z, x, post_g, next_g):
    m, d = x.shape
    tm = _row_tile(m, 208)
    row = pl.BlockSpec((tm, d), lambda i: (i, 0))
    vec = pl.BlockSpec((1, d), lambda i: (0, 0))
    return pl.pallas_call(
        _norm_residual_kernel,
        grid=(m // tm,),
        in_specs=[row, row, vec, vec],
        out_specs=[row, row],
        out_shape=[jax.ShapeDtypeStruct((m, d), F32), jax.ShapeDtypeStruct((m, d), BF16)],
        compiler_params=_cparams(("parallel",)),
        name="norm_residual",
    )(z, x, post_g.reshape(1, d), next_g.reshape(1, d))


def _matmul_kernel(x_ref, w_ref, o_ref):
    o_ref[...] = jnp.dot(x_ref[...], w_ref[...], preferred_element_type=F32).astype(o_ref.dtype)


def _matmul(x, w, tm, tn, out_dtype, name):
    m, k = x.shape
    n = w.shape[1]
    tm = _row_tile(m, tm)
    assert n % tn == 0, (n, tn)
    return pl.pallas_call(
        _matmul_kernel,
        grid=(m // tm, n // tn),
        in_specs=[pl.BlockSpec((tm, k), lambda i, j: (i, 0)),
                  pl.BlockSpec((k, tn), lambda i, j: (0, j))],
        out_specs=pl.BlockSpec((tm, tn), lambda i, j: (i, j)),
        out_shape=jax.ShapeDtypeStruct((m, n), out_dtype),
        compiler_params=_cparams(("parallel", "parallel")),
        name=name,
    )(x, w)


def _gate_merge_kernel(h_ref, o_ref, wg_ref, wb_ref, out_ref, acc_ref):
    n = pl.program_id(2)
    gate = _sigmoid(jnp.dot(h_ref[...], wg_ref[...], preferred_element_type=F32))
    y = jnp.dot(o_ref[...], wb_ref[...], preferred_element_type=F32)

    @pl.when(n == 0)
    def _():
        acc_ref[...] = gate * y

    @pl.when(n > 0)
    def _():
        acc_ref[...] += gate * y

    @pl.when(n == N_BRANCH - 1)
    def _():
        out_ref[...] = acc_ref[...].astype(out_ref.dtype)


def _gate_merge(h, o4, w_gate, w_branch, tn=512):
    m, d = h.shape
    tm = _row_tile(m, 1040)
    nj = d // tn
    return pl.pallas_call(
        _gate_merge_kernel,
        grid=(m // tm, nj, N_BRANCH),
        in_specs=[pl.BlockSpec((tm, d), lambda i, j, n: (i, 0)),
                  pl.BlockSpec((None, tm, MIX_W), lambda i, j, n: (n, i, 0)),
                  pl.BlockSpec((d, tn), lambda i, j, n: (0, n * nj + j)),
                  pl.BlockSpec((None, MIX_W, tn), lambda i, j, n: (n, 0, j))],
        out_specs=pl.BlockSpec((tm, tn), lambda i, j, n: (i, j)),
        out_shape=jax.ShapeDtypeStruct((m, d), BF16),
        scratch_shapes=[pltpu.VMEM((tm, tn), F32)],
        compiler_params=_cparams(("parallel", "parallel", "arbitrary")),
        name="gate_merge",
    )(h, o4, w_gate, w_branch)


def _ffn_up_kernel(h_ref, halo_ref, wa_ref, wb_ref, ca_ref, cb_ref, o_ref, *, tiles_per_seq):
    i = pl.program_id(0)
    first = (i % tiles_per_seq) == 0
    h = h_ref[...]
    halo = halo_ref[...]
    tm = h.shape[0]
    row = lax.broadcasted_iota(jnp.int32, (tm, 1), 0)

    def conv(w_ref, c_ref):
        u = jnp.dot(h, w_ref[...], preferred_element_type=F32)
        uh = jnp.dot(halo, w_ref[...], preferred_element_type=F32)
        uh = jnp.where(first, 0.0, uh)
        p1, p2 = uh[15:16], uh[14:15]
        u1 = jnp.where(row == 0, p1, pltpu.roll(u, 1, 0))
        u2 = pltpu.roll(u, 2, 0)
        u2 = jnp.where(row == 0, p2, jnp.where(row == 1, p1, u2))
        c = c_ref[...]
        return c[2:3] * u + c[1:2] * u1 + c[0:1] * u2

    a = conv(wa_ref, ca_ref)
    b = conv(wb_ref, cb_ref)
    o_ref[...] = (_silu(a) * b).astype(o_ref.dtype)


def _ffn_up(h, w_up, conv_w, lp, tn=256):
    m, d = h.shape
    tm = _row_tile(lp, 1040)
    nj = D_FF // tn
    halo_blk = tm // 16
    kern = functools.partial(_ffn_up_kernel, tiles_per_seq=lp // tm)
    return pl.pallas_call(
        kern,
        grid=(m // tm, nj),
        in_specs=[pl.BlockSpec((tm, d), lambda i, j: (i, 0)),
                  pl.BlockSpec((16, d), lambda i, j: (jnp.maximum(i * halo_blk - 1, 0), 0)),
                  pl.BlockSpec((d, tn), lambda i, j: (0, j)),
                  pl.BlockSpec((d, tn), lambda i, j: (0, nj + j)),
                  pl.BlockSpec((FFN_CONV, tn), lambda i, j: (0, j)),
                  pl.BlockSpec((FFN_CONV, tn), lambda i, j: (0, nj + j))],
        out_specs=pl.BlockSpec((tm, tn), lambda i, j: (i, j)),
        out_shape=jax.ShapeDtypeStruct((m, D_FF), BF16),
        compiler_params=_cparams(("parallel", "parallel")),
        name="ffn_up",
    )(h, h, w_up, w_up, conv_w, conv_w)


def _retention_kernel(q_ref, k_ref, v_ref, g_ref, cos_ref, sin_ref, lg_ref, gain_ref, o_ref, s_ref):
    r = pl.program_id(2)

    @pl.when(r == 0)
    def _():
        s_ref[...] = jnp.zeros_like(s_ref)

    half = RET_D // 2
    cos, sin = cos_ref[...], sin_ref[...]

    def rope(t):
        t1, t2 = t[:, :half], t[:, half:]
        return jnp.concatenate([t1 * cos - t2 * sin, t1 * sin + t2 * cos], axis=1)

    q = rope(q_ref[...]) * RET_D ** -0.5
    k = rope(k_ref[...])
    v = v_ref[...]
    n = q.shape[0]
    lg = lg_ref[...][:, :1]
    ri = lax.broadcasted_iota(jnp.int32, (n, n), 0)
    ci = lax.broadcasted_iota(jnp.int32, (n, n), 1)
    rel = (ri - ci).astype(F32)
    decay = jnp.where(rel >= 0, jnp.exp(lg * jnp.maximum(rel, 0.0)), 0.0)
    idx = lax.broadcasted_iota(jnp.int32, (n, 1), 0).astype(F32)
    q_decay = jnp.exp(lg * (idx + 1.0))
    k_decay = jnp.exp(lg * (n - 1.0 - idx))
    s = s_ref[...]
    scores = _bdot_nt(q, k) * decay
    o = _bdot(scores, v) + _bdot(q, s) * q_decay
    s_ref[...] = s * jnp.exp(lg * float(n)) + _bdot_tn(k * k_decay, v)
    o = o - jnp.mean(o, axis=-1, keepdims=True)
    o = o * lax.rsqrt(jnp.mean(o * o, axis=-1, keepdims=True) + EPS) * gain_ref[...]
    o_ref[...] = (o * _silu(g_ref[...])).astype(o_ref.dtype)


def _retention(proj, cos, sin, norm_gain, batch, lp):
    m = proj.shape[0]
    nr = lp // ROWS
    gamma = 1.0 - 2.0 ** (-5.0 - np.arange(RET_HEADS, dtype=np.float64))
    lg = jnp.asarray(np.broadcast_to(np.log(gamma)[:, None, None], (RET_HEADS, 1, LANE)), F32)

    def part(p):
        return pl.BlockSpec((ROWS, RET_D), lambda b, h, r: (b * nr + r, p * RET_HEADS + h))

    tab = pl.BlockSpec((ROWS, RET_D // 2), lambda b, h, r: (r, 0))
    return pl.pallas_call(
        _retention_kernel,
        grid=(batch, RET_HEADS, nr),
        in_specs=[part(0), part(1), part(2), part(3), tab, tab,
                  pl.BlockSpec((None, 1, LANE), lambda b, h, r: (h, 0, 0)),
                  pl.BlockSpec((1, RET_D), lambda b, h, r: (0, h))],
        out_specs=pl.BlockSpec((ROWS, RET_D), lambda b, h, r: (b * nr + r, h)),
        out_shape=jax.ShapeDtypeStruct((m, MIX_W), BF16),
        scratch_shapes=[pltpu.VMEM((RET_D, RET_D), F32)],
        compiler_params=_cparams(("parallel", "parallel", "arbitrary")),
        name="retention",
    )(proj, proj, proj, proj, cos, sin, lg, norm_gain.reshape(1, MIX_W))


GLA_LEVELS = 6


def _gla_consts():
    i, j = _np_masks()
    mats = [(j <= i), (j > i)]
    pair = []
    for lvl in range(GLA_LEVELS):
        s = 32 >> lvl
        blk = (i // (2 * s)) * (2 * s)
        mid = blk + s - 1
        second = (i - blk) >= s
        mats.append(second & (j > mid) & (j <= i))
        mats.append((~second) & (j > i) & (j <= mid))
        jb = (j // (2 * s)) * (2 * s)
        pair.append(second & (jb == blk) & ((j - jb) < s))
    pair.append(i == j)
    em = np.concatenate(mats, axis=0).astype(np.float32)
    pm = np.stack(pair, axis=0).astype(np.float32)
    return jnp.asarray(em, BF16), jnp.asarray(pm, F32)


def _gla_kernel(q_ref, k_ref, v_ref, g_ref, lr_ref, w2_ref, bias_ref, gain_ref, em_ref, pm_ref,
                o_ref, s_ref):
    @pl.when(pl.program_id(1) == 0)
    def _():
        s_ref[...] = jnp.zeros_like(s_ref)

    heads = range(GLA_HEADS)
    ks = [slice(h * GLA_DK, (h + 1) * GLA_DK) for h in heads]
    vs = [slice(h * GLA_DV, (h + 1) * GLA_DV) for h in heads]
    x = _bdot(lr_ref[...], w2_ref[...]) + bias_ref[...]
    la = (jnp.minimum(x, 0.0) - jnp.log1p(jnp.exp(-jnp.abs(x)))) / GLA_TAU
    ex = jnp.exp(_mask_dot(em_ref[...], la))
    e_b = ex[0:CH]
    e_rest = ex[CH:2 * CH]
    q = q_ref[...] * GLA_DK ** -0.5
    k = k_ref[...]
    v = [v_ref[:, s] for s in vs]
    scores = [_bdot_nt(q[:, s], k[:, s]) * pm_ref[GLA_LEVELS] for s in ks]
    for lvl in range(GLA_LEVELS):
        qe = q * ex[(2 + 2 * lvl) * CH:(3 + 2 * lvl) * CH]
        ke = k * ex[(3 + 2 * lvl) * CH:(4 + 2 * lvl) * CH]
        pm = pm_ref[lvl]
        scores = [scores[h] + _bdot_nt(qe[:, ks[h]], ke[:, ks[h]]) * pm for h in heads]
    st = [s_ref[h] for h in heads]
    qb = q * e_b
    kr = k * e_rest
    o = [_bdot(scores[h], v[h]) + _bdot_nt(qb[:, ks[h]], st[h]) for h in heads]
    upd = [_bdot_tn(v[h], kr[:, ks[h]]) for h in heads]
    for h in heads:
        s_ref[h] = st[h] * e_b[CH - 1:CH, ks[h]] + upd[h]
        oh = o[h] * lax.rsqrt(jnp.mean(o[h] * o[h], axis=-1, keepdims=True) + EPS) * gain_ref[:, vs[h]]
        o_ref[:, vs[h]] = (oh * _silu(g_ref[:, vs[h]])).astype(o_ref.dtype)


def _gla(proj, w2p, bias, norm_gain, batch, lp):
    m = proj.shape[0]
    nc = lp // CH
    em, pm = _gla_consts()
    qk_w = GLA_HEADS * GLA_DK
    const = lambda shape: pl.BlockSpec(shape, lambda b, c: (0,) * len(shape))
    return pl.pallas_call(
        _gla_kernel,
        grid=(batch, nc),
        in_specs=[pl.BlockSpec((CH, qk_w), lambda b, c: (b * nc + c, COL_GLA_QK // qk_w)),
                  pl.BlockSpec((CH, qk_w), lambda b, c: (b * nc + c, COL_GLA_QK // qk_w + 1)),
                  pl.BlockSpec((CH, MIX_W), lambda b, c: (b * nc + c, COL_GLA_VG // MIX_W)),
                  pl.BlockSpec((CH, MIX_W), lambda b, c: (b * nc + c, COL_GLA_VG // MIX_W + 1)),
                  pl.BlockSpec((CH, LANE), lambda b, c: (b * nc + c, COL_GLA_LR // LANE)),
                  const((LANE, qk_w)), const((1, qk_w)), const((1, MIX_W)),
                  const(em.shape), const(pm.shape)],
        out_specs=pl.BlockSpec((CH, MIX_W), lambda b, c: (b * nc + c, 0)),
        out_shape=jax.ShapeDtypeStruct((m, MIX_W), BF16),
        scratch_shapes=[pltpu.VMEM((GLA_HEADS, GLA_DV, GLA_DK), F32)],
        compiler_params=_cparams(("parallel", "arbitrary")),
        name="gla",
    )(proj, proj, proj, proj, proj, w2p, bias.reshape(1, -1), norm_gain.reshape(1, MIX_W), em, pm)


def _gdn_kernel(q_ref, k_ref, v_ref, z_ref, ab_ref, cq_ref, ck_ref, cv_ref, alog_ref, dtb_ref,
                gain_ref, tri_ref, blk_ref, o_ref, s_ref, carry_ref):
    @pl.when(pl.program_id(1) == 0)
    def _():
        s_ref[...] = jnp.zeros_like(s_ref)
        carry_ref[...] = jnp.zeros_like(carry_ref)

    def conv_silu(x_ref, c_ref, slot):
        x = x_ref[...]
        ext = jnp.concatenate([carry_ref[slot], x], axis=0)
        carry_ref[slot] = x[CH - 8:, :]
        c = c_ref[...]
        y = c[3:4] * x
        for t in range(1, GDN_CONV):
            y = y + c[3 - t:4 - t] * ext[8 - t:8 - t + CH, :]
        return _silu(y)

    def l2norm(t):
        return t * lax.rsqrt(jnp.sum(t * t, axis=-1, keepdims=True) + EPS)

    qc = conv_silu(q_ref, cq_ref, 0)
    kc = conv_silu(k_ref, ck_ref, 1)
    vc = conv_silu(v_ref, cv_ref, 2)
    ab = ab_ref[...]
    g_t = -jnp.exp(alog_ref[...]) * _softplus(ab + dtb_ref[...])
    beta_t = _sigmoid(ab)
    rep = lambda t, idx: jnp.broadcast_to(t[:, idx:idx + 1], (CH, LANE))
    heads = range(GDN_HEADS)
    hs = [slice(h * GDN_D, (h + 1) * GDN_D) for h in heads]
    gcum_all = _mask_dot(tri_ref[...], jnp.concatenate([rep(g_t, h) for h in heads], axis=1))

    same_blk = blk_ref[...]
    ri = lax.broadcasted_iota(jnp.int32, (CH, CH), 0)
    ci = lax.broadcasted_iota(jnp.int32, (CH, CH), 1)
    eye = (ri == ci).astype(F32)
    q = [l2norm(qc[:, s]) * GDN_D ** -0.5 for s in hs]
    k = [l2norm(kc[:, s]) for s in hs]
    v = [vc[:, s] for s in hs]
    beta = [rep(beta_t, GDN_HEADS + h) for h in heads]
    gcum = [gcum_all[:, s] for s in hs]
    grow = [jnp.transpose(g)[:1, :CH] for g in gcum]
    decay = [jnp.where(ri >= ci, jnp.exp(jnp.minimum(gcum[h][:, :1] - grow[h], 0.0)), 0.0) for h in heads]
    kb = [k[h] * beta[h] for h in heads]
    kq = [_bdot_nt(jnp.concatenate([kb[h], q[h]], axis=0), k[h]) for h in heads]
    a_mat = [jnp.where(ri > ci, kq[h][:CH] * decay[h], 0.0) for h in heads]
    attn = [kq[h][CH:] * decay[h] for h in heads]
    t_inv = _inv_unit_lower(a_mat, same_blk, eye)
    e_g = [jnp.exp(g) for g in gcum]
    sol = [_bdot(t_inv[h], jnp.concatenate([v[h] * beta[h], kb[h] * e_g[h]], axis=1)) for h in heads]
    s = [s_ref[h] for h in heads]
    m = [_bdot(jnp.concatenate([sol[h][:, GDN_D:], q[h] * e_g[h]], axis=0), s[h]) for h in heads]
    v_new = [sol[h][:, :GDN_D] - m[h][:CH] for h in heads]
    o = [m[h][CH:] + _bdot(attn[h], v_new[h]) for h in heads]
    g_last = [g[CH - 1:CH, :] for g in gcum]
    upd = [_bdot_tn(k[h] * jnp.exp(g_last[h] - gcum[h]), v_new[h]) for h in heads]
    gain = gain_ref[...]
    for h in heads:
        s_ref[h] = s[h] * jnp.exp(g_last[h]) + upd[h]
        oh = o[h] * lax.rsqrt(jnp.mean(o[h] * o[h], axis=-1, keepdims=True) + EPS) * gain
        o_ref[:, hs[h]] = (oh * _silu(z_ref[:, hs[h]])).astype(o_ref.dtype)


def _gdn(proj, conv_w, a_log, dt_bias, norm_gain, batch, lp):
    m = proj.shape[0]
    nc = lp // CH
    nh = GDN_HEADS
    pad = lambda t: jnp.pad(t.reshape(1, nh), ((0, 0), (0, LANE - nh)))
    part = lambda p: pl.BlockSpec((CH, MIX_W), lambda b, c: (b * nc + c, COL_GDN // MIX_W + p))
    cpart = lambda p: pl.BlockSpec((GDN_CONV, MIX_W), lambda b, c: (0, p))
    const = lambda shape: pl.BlockSpec(shape, lambda b, c: (0,) * len(shape))
    return pl.pallas_call(
        _gdn_kernel,
        grid=(batch, nc),
        in_specs=[part(0), part(1), part(2), part(3),
                  pl.BlockSpec((CH, LANE), lambda b, c: (b * nc + c, COL_GDN_AB // LANE)),
                  cpart(0), cpart(1), cpart(2),
                  const((1, LANE)), const((1, LANE)), const((1, GDN_D)),
                  const((CH, CH)), const((CH, CH))],
        out_specs=pl.BlockSpec((CH, MIX_W), lambda b, c: (b * nc + c, 0)),
        out_shape=jax.ShapeDtypeStruct((m, MIX_W), BF16),
        scratch_shapes=[pltpu.VMEM((nh, GDN_D, GDN_D), F32), pltpu.VMEM((3, 8, MIX_W), F32)],
        compiler_params=_cparams(("parallel", "arbitrary")),
        name="gdn",
    )(proj, proj, proj, proj, proj, conv_w, conv_w, conv_w, pad(a_log), pad(dt_bias),
      norm_gain.reshape(1, GDN_D), _const_tri(), _const_same_blk())


def _rwkv_prep_kernel(sr_ref, sk_ref, sv_ref, swa_ref, sg_ref, mu_ref, w0_ref, a0_ref, kk_ref, ka_ref,
                      w2_ref, a2_ref, g2_ref, seg_ref, segt_ref,
                      r_ref, lw_ref, k_ref, v_ref, na_ref, bb_ref, g_ref, carry_ref):
    rb = pl.program_id(1)

    @pl.when(rb == 0)
    def _():
        carry_ref[...] = jnp.zeros_like(carry_ref)

    x = jnp.concatenate([sr_ref[...], sk_ref[...], sv_ref[...], swa_ref[...], sg_ref[...]], axis=1)
    ext = jnp.concatenate([carry_ref[...], x], axis=0)
    carry_ref[...] = x[ROWS - 8:, :]
    prev = ext[7:7 + ROWS, :]
    s = x + (prev - x) * mu_ref[...]
    r = s[:, :MIX_W]
    k = s[:, MIX_W:2 * MIX_W]
    v = s[:, 2 * MIX_W:3 * MIX_W]
    wa_src = s[:, RWKV_LR:RWKV_LR + LANE]
    g_src = s[:, RWKV_LR + LANE:]
    lw_ref[...] = -RWKV_DECAY_SCALE * _sigmoid(w0_ref[...] + _bdot(jnp.tanh(wa_src), w2_ref[...]))
    a = _sigmoid(a0_ref[...] + _bdot(wa_src, a2_ref[...]))
    g_ref[...] = _bdot(_sigmoid(g_src), g2_ref[...])
    kk = k * kk_ref[...]
    sq = kk * kk
    sq_hi = sq.astype(BF16)
    sq_lo = (sq - sq_hi.astype(F32)).astype(BF16)
    seg = seg_ref[...]
    hsum = (jnp.dot(sq_hi, seg, preferred_element_type=F32)
            + jnp.dot(sq_lo, seg, preferred_element_type=F32))
    inv = lax.rsqrt(hsum + EPS)
    inv_hi = inv.astype(BF16)
    inv_lo = (inv - inv_hi.astype(F32)).astype(BF16)
    segt = segt_ref[...]
    inv_full = (jnp.dot(inv_hi, segt, preferred_element_type=F32)
                + jnp.dot(inv_lo, segt, preferred_element_type=F32))
    kk = kk * inv_full
    r_ref[...] = r
    k_ref[...] = k * (1.0 + (a - 1.0) * ka_ref[...])
    v_ref[...] = v
    na_ref[...] = -kk
    bb_ref[...] = kk * a


def _rwkv_scan_kernel(r_ref, lw_ref, k_ref, v_ref, na_ref, bb_ref, g_ref, rk_ref, lnw_ref, lnb_ref,
                      cm_ref, blk_ref, o_ref, h_ref):
    @pl.when(pl.program_id(1) == 0)
    def _():
        h_ref[...] = jnp.zeros_like(h_ref)

    n = RWKV_N
    npair = MIX_W // LANE
    ri = lax.broadcasted_iota(jnp.int32, (CH, CH), 0)
    ci = lax.broadcasted_iota(jnp.int32, (CH, CH), 1)
    eye = (ri == ci).astype(F32)
    strict = ri > ci
    ri2 = lax.broadcasted_iota(jnp.int32, (CH, LANE), 0)
    ci2 = lax.broadcasted_iota(jnp.int32, (CH, LANE), 1) % CH
    incl2 = ri2 >= ci2
    lo = lax.broadcasted_iota(jnp.int32, (1, LANE), 1) < n
    vi = lax.broadcasted_iota(jnp.int32, (LANE, LANE), 0) // n
    ki = lax.broadcasted_iota(jnp.int32, (LANE, LANE), 1) // n
    same_head = (vi == ki).astype(F32)
    same_blk = blk_ref[...]

    r, lw, k, v = r_ref[...], lw_ref[...], k_ref[...], v_ref[...]
    na, bb = na_ref[...], bb_ref[...]
    cums = _mask_dot(cm_ref[...], lw)
    cum, rest = cums[:CH], cums[CH:]
    e_cum = jnp.exp(cum)
    e_neg = jnp.exp(-cum)
    e_rest = jnp.exp(rest)
    rt = r * e_cum
    at = na * jnp.exp(cum - lw)
    bt = bb * e_neg
    kt = k * e_neg
    bh = bb * e_rest
    kh = k * e_rest

    pairs = [slice(p * LANE, (p + 1) * LANE) for p in range(npair)]
    pm = []
    for ps in pairs:
        a_p, r_p = at[:, ps], rt[:, ps]
        lhs = jnp.concatenate([jnp.where(lo, a_p, 0.0), jnp.where(lo, r_p, 0.0),
                               jnp.where(lo, 0.0, a_p), jnp.where(lo, 0.0, r_p)], axis=0)
        pm.append(_bdot_nt(lhs, jnp.concatenate([bt[:, ps], kt[:, ps]], axis=0)))
    heads = [(p, s) for p in range(npair) for s in range(2)]
    a_ab = [jnp.where(strict, pm[p][2 * s * CH:(2 * s + 1) * CH, :CH], 0.0) for p, s in heads]
    a_ak = [jnp.where(strict, pm[p][2 * s * CH:(2 * s + 1) * CH, CH:], 0.0) for p, s in heads]
    a_rbk = [jnp.where(incl2, pm[p][(2 * s + 1) * CH:(2 * s + 2) * CH, :], 0.0) for p, s in heads]
    t_inv = _inv_unit_lower([-a for a in a_ab], same_blk, eye)
    akv = [_bdot(a_ak[i], v[:, pairs[p]]) for i, (p, s) in enumerate(heads)]
    sol = [_bdot(t_inv[i], jnp.concatenate([at[:, pairs[p]], akv[i]], axis=1))
           for i, (p, s) in enumerate(heads)]
    sol = [jnp.where(jnp.concatenate([lo, lo], axis=1), sol[2 * p], sol[2 * p + 1]) for p in range(npair)]
    ht = [h_ref[p] for p in range(npair)]
    m = [_bdot_nt(jnp.concatenate([sol[p][:, :LANE], rt[:, pairs[p]]], axis=0), ht[p]) for p in range(npair)]
    uv = [jnp.concatenate([m[p][:CH] + sol[p][:, LANE:], v[:, pairs[p]]], axis=0) for p in range(npair)]
    yy = [_bdot(a_rbk[i], uv[p]) for i, (p, s) in enumerate(heads)]
    upd = [_bdot_tn(uv[p], jnp.concatenate([bh[:, pairs[p]], kh[:, pairs[p]]], axis=0)) for p in range(npair)]
    for p, ps in enumerate(pairs):
        h_ref[p] = ht[p] * e_cum[CH - 1:CH, ps] + upd[p] * same_head
        y = m[p][CH:] + jnp.where(lo, yy[2 * p], yy[2 * p + 1])

        def head_sum(t):
            s0 = jnp.sum(jnp.where(lo, t, 0.0), axis=-1, keepdims=True)
            s1 = jnp.sum(jnp.where(lo, 0.0, t), axis=-1, keepdims=True)
            return jnp.where(lo, s0, s1)

        y = y - head_sum(y) * (1.0 / n)
        y = y * lax.rsqrt(head_sum(y * y) * (1.0 / n) + RWKV_GN_EPS)
        y = y * lnw_ref[:, ps] + lnb_ref[:, ps]
        bonus = head_sum(r[:, ps] * k[:, ps] * rk_ref[:, ps])
        o_ref[:, ps] = ((y + bonus * v[:, ps]) * g_ref[:, ps]).astype(o_ref.dtype)


def _rwkv(proj, mu_p, w0, w2p, a0, a2p, g2p, kk, ka, rk, ln_w, ln_b, batch, lp):
    m = proj.shape[0]
    nr = lp // ROWS
    c = np.arange(MIX_W)[:, None] // RWKV_N == np.arange(LANE)[None, :]
    seg = jnp.asarray(c.astype(np.float32), BF16)
    segt = jnp.asarray(c.T.astype(np.float32), BF16)
    vec = lambda t: t.reshape(1, -1)
    const2 = lambda shape: pl.BlockSpec(shape, lambda b, r: (0, 0))
    row_w = pl.BlockSpec((ROWS, MIX_W), lambda b, r: (b * nr + r, 0))
    wide = jax.ShapeDtypeStruct((m, MIX_W), F32)
    src = lambda width, col: pl.BlockSpec((ROWS, width), lambda b, r: (b * nr + r, col // width))
    r_, lw, k_, v_, na, bb, g = pl.pallas_call(
        _rwkv_prep_kernel,
        grid=(batch, nr),
        in_specs=[src(MIX_W, COL_RWKV), src(MIX_W, COL_RWKV + MIX_W), src(MIX_W, COL_RWKV + 2 * MIX_W),
                  src(LANE, COL_RWKV_WA), src(2 * LANE, COL_RWKV_G),
                  const2((1, RWKV_PW)), const2((1, MIX_W)), const2((1, MIX_W)), const2((1, MIX_W)),
                  const2((1, MIX_W)), const2((LANE, MIX_W)), const2((LANE, MIX_W)),
                  const2((2 * LANE, MIX_W)), const2((MIX_W, LANE)), const2((LANE, MIX_W))],
        out_specs=[row_w] * 7,
        out_shape=[wide] * 7,
        scratch_shapes=[pltpu.VMEM((8, RWKV_PW), F32)],
        compiler_params=_cparams(("parallel", "arbitrary")),
        name="rwkv_prep",
    )(proj, proj, proj, proj, proj, vec(mu_p), vec(w0), vec(a0), vec(kk), vec(ka), w2p, a2p, g2p, seg, segt)

    i, j = _np_masks()
    cm = jnp.asarray(np.concatenate([(j <= i), (j > i)], axis=0).astype(np.float32), BF16)
    nc = lp // CH
    chunk = pl.BlockSpec((CH, MIX_W), lambda b, c: (b * nc + c, 0))
    return pl.pallas_call(
        _rwkv_scan_kernel,
        grid=(batch, nc),
        in_specs=[chunk] * 7 + [const2((1, MIX_W))] * 3 + [const2((2 * CH, CH)), const2((CH, CH))],
        out_specs=chunk,
        out_shape=jax.ShapeDtypeStruct((m, MIX_W), BF16),
        scratch_shapes=[pltpu.VMEM((MIX_W // LANE, LANE, LANE), F32)],
        compiler_params=_cparams(("parallel", "arbitrary")),
        name="rwkv_scan",
    )(r_, lw, k_, v_, na, bb, g, vec(rk), vec(ln_w), vec(ln_b), cm, _const_same_blk())


def _pad_cols(w, width):
    return jnp.pad(w, ((0, 0), (0, width - w.shape[1])))


def _pad_rows(w, rows, offset=0):
    return jnp.pad(w, ((offset, rows - offset - w.shape[0]), (0, 0)))


def _reorder_w_in(w_in):
    ret_w = 4 * MIX_W
    gla = ret_w
    gla_v = gla + 2 * GLA_HEADS * GLA_DK
    gla_lr = gla_v + 2 * GLA_HEADS * GLA_DV
    gdn = gla_lr + GLA_RANK
    gdn_ab = gdn + 4 * MIX_W
    rwkv = gdn_ab + 2 * GDN_HEADS
    rwkv_wa = rwkv + 3 * MIX_W
    rwkv_g = rwkv_wa + RWKV_W_RANK + RWKV_A_RANK
    pieces = [w_in[:, :ret_w], w_in[:, gdn:gdn_ab], w_in[:, rwkv:rwkv_wa], w_in[:, gla_v:gla_lr],
              w_in[:, gla:gla_v], _pad_cols(w_in[:, gdn_ab:rwkv], LANE), w_in[:, rwkv_wa:rwkv_g],
              _pad_cols(w_in[:, rwkv_g:], 2 * LANE), _pad_cols(w_in[:, gla_lr:gdn], LANE)]
    out = jnp.concatenate(pieces, axis=1).astype(BF16)
    assert out.shape[1] == PROJ_W, out.shape
    return out


def _mixer_sublayer(x, h, lyr, cos, sin, batch, lp, next_g):
    proj = _matmul(h, _reorder_w_in(lyr["w_in"]), 1040, 1152, F32, "proj")

    o_a = _retention(proj, cos, sin, lyr["ret_norm"], batch, lp)
    o_b = _gla(proj, _pad_rows(lyr["gla_w2"], LANE).astype(BF16), lyr["gla_b"], lyr["gla_norm"],
               batch, lp)
    o_c = _gdn(proj, lyr["gdn_conv"], lyr["gdn_a_log"], lyr["gdn_dt_bias"], lyr["gdn_norm"], batch, lp)
    mu_p = jnp.pad(lyr["rwkv_mu"], (0, RWKV_PW - lyr["rwkv_mu"].shape[0]))
    w2p = _pad_rows(lyr["rwkv_w2"], LANE).astype(BF16)
    a2p = _pad_rows(lyr["rwkv_a2"], LANE, RWKV_W_RANK).astype(BF16)
    g2p = _pad_rows(lyr["rwkv_g2"], 2 * LANE).astype(BF16)
    o_d = _rwkv(proj, mu_p, lyr["rwkv_w0"], w2p, lyr["rwkv_a0"], a2p, g2p, lyr["rwkv_kk"],
                lyr["rwkv_ka"], lyr["rwkv_rk"], lyr["rwkv_ln_w"], lyr["rwkv_ln_b"], batch, lp)

    o4 = jnp.stack([o_a, o_b, o_c, o_d], axis=0)
    merged = _gate_merge(h, o4, lyr["w_gate"].astype(BF16), lyr["w_branch"].astype(BF16))
    z = _matmul(merged, lyr["w_out"].astype(BF16), 1040, 1024, F32, "w_out")
    return _norm_residual(z, x, lyr["post_mix"], next_g)


def _ffn_sublayer(x, h, lyr, lp, next_g):
    act = _ffn_up(h, lyr["w_up"].astype(BF16), lyr["ffn_conv"], lp)
    z = _matmul(act, lyr["w_down"].astype(BF16), 520, 512, F32, "ffn_down")
    return _norm_residual(z, x, lyr["post_ffn"], next_g)


_LAYER_KEYS = ("pre_mix", "w_in", "ret_norm", "gla_w2", "gla_b", "gla_norm", "gdn_conv", "gdn_a_log",
               "gdn_dt_bias", "gdn_norm", "rwkv_mu", "rwkv_w0", "rwkv_w2", "rwkv_a0", "rwkv_a2", "rwkv_g2",
               "rwkv_kk", "rwkv_ka", "rwkv_rk", "rwkv_ln_w", "rwkv_ln_b", "w_branch", "w_gate", "w_out",
               "post_mix", "pre_ffn", "w_up", "ffn_conv", "w_down", "post_ffn")


def _trunk(x, meta, params):
    batch, seq, d = x.shape
    depth = params["pre_mix"].shape[0]
    l = N_META + seq
    lp = -(-l // ROWS) * ROWS
    hcat = jnp.concatenate([jnp.broadcast_to(meta.astype(x.dtype)[None], (batch, N_META, d)), x,
                            jnp.zeros((batch, lp - l, d), x.dtype)], axis=1)
    xr = hcat.reshape(batch * lp, d)

    half = RET_D // 2
    inv_freq = ROPE_BASE ** (-jnp.arange(half, dtype=F32) / half)
    ang = jnp.arange(lp, dtype=F32)[:, None] * inv_freq[None, :]
    cos, sin = jnp.cos(ang), jnp.sin(ang)

    layers = [{k: params[k][i] for k in _LAYER_KEYS} for i in range(depth)]
    h = _prenorm(xr, layers[0]["pre_mix"])
    for i, lyr in enumerate(layers):
        xr, h = _mixer_sublayer(xr, h, lyr, cos, sin, batch, lp, lyr["pre_ffn"])
        next_g = layers[i + 1]["pre_mix"] if i + 1 < depth else lyr["pre_ffn"]
        xr, h = _ffn_sublayer(xr, h, lyr, lp, next_g)
    return xr.reshape(batch, lp, d)[:, N_META:l]


def kernel(x, meta, pre_mix, w_in, ret_norm, gla_w2, gla_b, gla_norm, gdn_conv, gdn_a_log, gdn_dt_bias,
           gdn_norm, rwkv_mu, rwkv_w0, rwkv_w2, rwkv_a0, rwkv_a2, rwkv_g2, rwkv_kk, rwkv_ka, rwkv_rk,
           rwkv_ln_w, rwkv_ln_b, w_branch, w_gate, w_out, post_mix, pre_ffn, w_up, ffn_conv, w_down, post_ffn):
    params = dict(zip(_LAYER_KEYS, (pre_mix, w_in, ret_norm, gla_w2, gla_b, gla_norm, gdn_conv, gdn_a_log,
                                    gdn_dt_bias, gdn_norm, rwkv_mu, rwkv_w0, rwkv_w2, rwkv_a0, rwkv_a2,
                                    rwkv_g2, rwkv_kk, rwkv_ka, rwkv_rk, rwkv_ln_w, rwkv_ln_b, w_branch,
                                    w_gate, w_out, post_mix, pre_ffn, w_up, ffn_conv, w_down, post_ffn)))
    return _trunk(x, meta, params)
```

```python
import functools
import math

import numpy as np
import jax
import jax.numpy as jnp
from jax import lax
from jax.experimental import pallas as pl
from jax.experimental.pallas import tpu as pltpu

F32 = jnp.float32
BF16 = jnp.bfloat16

D_MODEL = 4096
N_META = 16
N_BRANCH = 4
MIX_W = D_MODEL // N_BRANCH
EPS = 1e-6
D_FF = 11008
FFN_CONV = 3

RET_HEADS, RET_D = 4, 256
ROPE_BASE = 10000.0
GLA_HEADS, GLA_DK, GLA_DV, GLA_RANK, GLA_TAU = 4, 128, 256, 16, 16.0
GDN_HEADS, GDN_D, GDN_CONV = 8, 128, 4
RWKV_HEADS, RWKV_N = 16, 64
RWKV_W_RANK, RWKV_A_RANK, RWKV_G_RANK = 64, 64, 160
RWKV_DECAY_SCALE = 0.606531
RWKV_GN_EPS = 64e-5

LANE = 128
CH = 64
SUB = 16
ROWS = 320
VMEM_LIMIT = 56 * 1024 * 1024

RWKV_LR = 3 * MIX_W
RWKV_PW = 3 * MIX_W + LANE + 2 * LANE
COL_RET = 0
COL_GDN = COL_RET + 4 * MIX_W
COL_RWKV = COL_GDN + 4 * MIX_W
COL_GLA_VG = COL_RWKV + 3 * MIX_W
COL_GLA_QK = COL_GLA_VG + 2 * MIX_W
COL_GDN_AB = COL_GLA_QK + 2 * GLA_HEADS * GLA_DK
COL_RWKV_WA = COL_GDN_AB + LANE
COL_RWKV_G = COL_RWKV_WA + LANE
COL_GLA_LR = COL_RWKV_G + 2 * LANE
PROJ_TN = 1024
PROJ_W = -(-(COL_GLA_LR + LANE) // PROJ_TN) * PROJ_TN


def _row_tile(m, pref):
    best = 16
    for t in range(16, min(m, pref) + 1, 16):
        if m % t == 0:
            best = t
    assert m % best == 0, (m, pref)
    return best


def _cparams(sem):
    return pltpu.CompilerParams(dimension_semantics=sem, vmem_limit_bytes=VMEM_LIMIT)


def _bdot(a, b):
    return jnp.dot(a.astype(BF16), b.astype(BF16), preferred_element_type=F32)


def _bdot_nt(a, b):
    return lax.dot_general(a.astype(BF16), b.astype(BF16), (((1,), (1,)), ((), ())),
                           preferred_element_type=F32)


def _bdot_tn(a, b):
    return lax.dot_general(a.astype(BF16), b.astype(BF16), (((0,), (0,)), ((), ())),
                           preferred_element_type=F32)


def _mask_dot(mask, x):
    hi = x.astype(BF16)
    lo = (x - hi.astype(F32)).astype(BF16)
    return (jnp.dot(mask, hi, preferred_element_type=F32)
            + jnp.dot(mask, lo, preferred_element_type=F32))


def _sigmoid(x):
    return 0.5 + 0.5 * jnp.tanh(0.5 * x)


def _silu(x):
    return x * _sigmoid(x)


def _softplus(x):
    return jnp.maximum(x, 0.0) + jnp.log1p(jnp.exp(-jnp.abs(x)))


def _inv_unit_lower(mats, same_blk, eye):
    each = lambda f, *ls: [f(*t) for t in zip(*ls)]
    d = [a * same_blk for a in mats]
    l = each(lambda a, dd: a - dd, mats, d)
    d2 = each(_bdot, d, d)
    d4 = each(_bdot, d2, d2)
    d8 = each(_bdot, d4, d4)
    p = [eye - dd for dd in d]
    p = each(lambda pp, m: pp + _bdot(pp, m), p, d2)
    p = each(lambda pp, m: pp + _bdot(pp, m), p, d4)
    td = each(lambda pp, m: pp + _bdot(pp, m), p, d8)
    x = each(_bdot, td, l)
    x2 = each(_bdot, x, x)
    q = [eye - xx for xx in x]
    q = each(lambda qq, m: qq + _bdot(qq, m), q, x2)
    return each(_bdot, q, td)


def _np_masks():
    i = np.arange(CH)[:, None]
    j = np.arange(CH)[None, :]
    return i, j


def _const_tri():
    i, j = _np_masks()
    return jnp.asarray((j <= i).astype(np.float32), BF16)


def _const_same_blk():
    i, j = _np_masks()
    return jnp.asarray((i // SUB == j // SUB).astype(np.float32), F32)


def _prenorm_kernel(x_ref, g_ref, o_ref):
    x = x_ref[...]
    y = x * lax.rsqrt(jnp.mean(x * x, axis=-1, keepdims=True) + EPS)
    o_ref[...] = (y * g_ref[...]).astype(o_ref.dtype)


def _prenorm(x, g):
    m, d = x.shape
    tm = _row_tile(m, 416)
    return pl.pallas_call(
        _prenorm_kernel,
        grid=(m // tm,),
        in_specs=[pl.BlockSpec((tm, d), lambda i: (i, 0)),
                  pl.BlockSpec((1, d), lambda i: (0, 0))],
        out_specs=pl.BlockSpec((tm, d), lambda i: (i, 0)),
        out_shape=jax.ShapeDtypeStruct((m, d), BF16),
        compiler_params=_cparams(("parallel",)),
        name="prenorm",
    )(x, g.reshape(1, d))


def _norm_residual_kernel(z_ref, x_ref, pg_ref, ng_ref, xo_ref, ho_ref):
    z = z_ref[...]
    y = z * lax.rsqrt(jnp.mean(z * z, axis=-1, keepdims=True) + EPS) * pg_ref[...]
    xn = x_ref[...] + y
    xo_ref[...] = xn
    h = xn * lax.rsqrt(jnp.mean(xn * xn, axis=-1, keepdims=True) + EPS) * ng_ref[...]
    ho_ref[...] = h.astype(ho_ref.dtype)


def _norm_residual(z, x, post_g, next_g):
    m, d = x.shape
    tm = _row_tile(m, 208)
    row = pl.BlockSpec((tm, d), lambda i: (i, 0))
    vec = pl.BlockSpec((1, d), lambda i: (0, 0))
    return pl.pallas_call(
        _norm_residual_kernel,
        grid=(m // tm,),
        in_specs=[row, row, vec, vec],
        out_specs=[row, row],
        out_shape=[jax.ShapeDtypeStruct((m, d), F32), jax.ShapeDtypeStruct((m, d), BF16)],
        compiler_params=_cparams(("parallel",)),
        name="norm_residual",
    )(z, x, post_g.reshape(1, d), next_g.reshape(1, d))


def _matmul_kernel(x_ref, w_ref, o_ref):
    o_ref[...] = jnp.dot(x_ref[...], w_ref[...], preferred_element_type=F32).astype(o_ref.dtype)


def _matmul(x, w, layer, tm, tn, out_dtype, name):
    m, k = x.shape
    n = w.shape[2]
    tm = _row_tile(m, tm)
    assert n % tn == 0, (n, tn)
    return pl.pallas_call(
        _matmul_kernel,
        grid=(m // tm, n // tn),
        in_specs=[pl.BlockSpec((tm, k), lambda i, j: (i, 0)),
                  pl.BlockSpec((None, k, tn), lambda i, j: (layer, 0, j))],
        out_specs=pl.BlockSpec((tm, tn), lambda i, j: (i, j)),
        out_shape=jax.ShapeDtypeStruct((m, n), out_dtype),
        compiler_params=_cparams(("parallel", "parallel")),
        name=name,
    )(x, w)


def _gate_merge_kernel(h_ref, o_ref, wg_ref, wb_ref, out_ref, acc_ref):
    n = pl.program_id(2)
    gate = _sigmoid(jnp.dot(h_ref[...], wg_ref[...], preferred_element_type=F32))
    y = jnp.dot(o_ref[...], wb_ref[...], preferred_element_type=F32)

    @pl.when(n == 0)
    def _():
        acc_ref[...] = gate * y

    @pl.when(n > 0)
    def _():
        acc_ref[...] += gate * y

    @pl.when(n == N_BRANCH - 1)
    def _():
        out_ref[...] = acc_ref[...].astype(out_ref.dtype)


def _gate_merge(h, o4, w_gate, w_branch, layer, tn=1024):
    m, d = h.shape
    tm = _row_tile(m, 1040)
    nj = d // tn
    return pl.pallas_call(
        _gate_merge_kernel,
        grid=(m // tm, nj, N_BRANCH),
        in_specs=[pl.BlockSpec((tm, d), lambda i, j, n: (i, 0), pipeline_mode=pl.Buffered(1)),
                  pl.BlockSpec((None, tm, MIX_W), lambda i, j, n: (n, i, 0)),
                  pl.BlockSpec((None, d, tn), lambda i, j, n: (layer, 0, n * nj + j)),
                  pl.BlockSpec((None, None, MIX_W, tn), lambda i, j, n: (layer, n, 0, j))],
        out_specs=pl.BlockSpec((tm, tn), lambda i, j, n: (i, j)),
        out_shape=jax.ShapeDtypeStruct((m, d), BF16),
        scratch_shapes=[pltpu.VMEM((tm, tn), F32)],
        compiler_params=_cparams(("parallel", "parallel", "arbitrary")),
        name="gate_merge",
    )(h, o4, w_gate, w_branch)


def _ffn_up_kernel(h_ref, halo_ref, wa_ref, wb_ref, ca_ref, cb_ref, o_ref, *, tiles_per_seq):
    i = pl.program_id(0)
    first = (i % tiles_per_seq) == 0
    h = h_ref[...]
    halo = halo_ref[...]
    row8 = lax.broadcasted_iota(jnp.int32, (8, 1), 0)

    def conv(w_ref, c_ref):
        u = jnp.dot(h, w_ref[...], preferred_element_type=F32)
        uh = jnp.dot(halo, w_ref[...], preferred_element_type=F32)
        uh = jnp.where(first, 0.0, uh)
        p1, p2 = uh[15:16], uh[14:15]
        u1 = pltpu.roll(u, 1, 0)
        u2 = pltpu.roll(u, 2, 0)
        c = c_ref[...]
        y = c[2:3] * u + c[1:2] * u1 + c[0:1] * u2
        f1 = jnp.where(row8 == 0, p1, u1[:8])
        f2 = jnp.where(row8 == 0, p2, jnp.where(row8 == 1, p1, u2[:8]))
        top = c[2:3] * u[:8] + c[1:2] * f1 + c[0:1] * f2
        return jnp.concatenate([top, y[8:]], axis=0)

    a = _silu(conv(wa_ref, ca_ref))
    b = conv(wb_ref, cb_ref)
    o_ref[...] = (a * b).astype(o_ref.dtype)


def _ffn_up(h, w_up, conv_w, layer, lp, tn=256):
    m, d = h.shape
    tm = _row_tile(lp, 2080)
    nj = D_FF // tn
    halo_blk = tm // 16
    kern = functools.partial(_ffn_up_kernel, tiles_per_seq=lp // tm)
    return pl.pallas_call(
        kern,
        grid=(m // tm, nj),
        in_specs=[pl.BlockSpec((tm, d), lambda i, j: (i, 0), pipeline_mode=pl.Buffered(1)),
                  pl.BlockSpec((16, d), lambda i, j: (jnp.maximum(i * halo_blk - 1, 0), 0)),
                  pl.BlockSpec((None, d, tn), lambda i, j: (layer, 0, j)),
                  pl.BlockSpec((None, d, tn), lambda i, j: (layer, 0, nj + j)),
                  pl.BlockSpec((None, FFN_CONV, tn), lambda i, j: (layer, 0, j)),
                  pl.BlockSpec((None, FFN_CONV, tn), lambda i, j: (layer, 0, nj + j))],
        out_specs=pl.BlockSpec((tm, tn), lambda i, j: (i, j)),
        out_shape=jax.ShapeDtypeStruct((m, D_FF), BF16),
        compiler_params=_cparams(("parallel", "parallel")),
        name="ffn_up",
    )(h, h, w_up, w_up, conv_w, conv_w)


def _retention_kernel(q_ref, k_ref, v_ref, g_ref, cos_ref, sin_ref, lg_ref, gain_ref, o_ref, s_ref):
    r = pl.program_id(2)

    @pl.when(r == 0)
    def _():
        s_ref[...] = jnp.zeros_like(s_ref)

    half = RET_D // 2
    cos, sin = cos_ref[...], sin_ref[...]

    def rope(t):
        t1, t2 = t[:, :half], t[:, half:]
        return jnp.concatenate([t1 * cos - t2 * sin, t1 * sin + t2 * cos], axis=1)

    q = rope(q_ref[...]) * RET_D ** -0.5
    k = rope(k_ref[...])
    v = v_ref[...]
    n = q.shape[0]
    lg = lg_ref[...][:, :1]
    ri = lax.broadcasted_iota(jnp.int32, (n, n), 0)
    ci = lax.broadcasted_iota(jnp.int32, (n, n), 1)
    rel = (ri - ci).astype(F32)
    decay = jnp.where(rel >= 0, jnp.exp(lg * jnp.maximum(rel, 0.0)), 0.0)
    idx = lax.broadcasted_iota(jnp.int32, (n, 1), 0).astype(F32)
    q_decay = jnp.exp(lg * (idx + 1.0))
    k_decay = jnp.exp(lg * (n - 1.0 - idx))
    s = s_ref[...]
    scores = _bdot_nt(q, k) * decay
    o = _bdot(scores, v) + _bdot(q, s) * q_decay
    s_ref[...] = s * jnp.exp(lg * float(n)) + _bdot_tn(k * k_decay, v)
    o = o - jnp.mean(o, axis=-1, keepdims=True)
    o = o * lax.rsqrt(jnp.mean(o * o, axis=-1, keepdims=True) + EPS) * gain_ref[...]
    o_ref[...] = (o * _silu(g_ref[...])).astype(o_ref.dtype)


def _retention(proj, cos, sin, norm_gain, batch, lp):
    m = proj.shape[0]
    nr = lp // ROWS
    gamma = 1.0 - 2.0 ** (-5.0 - np.arange(RET_HEADS, dtype=np.float64))
    lg = jnp.asarray(np.broadcast_to(np.log(gamma)[:, None, None], (RET_HEADS, 1, LANE)), F32)

    def part(p):
        return pl.BlockSpec((ROWS, RET_D), lambda b, h, r: (b * nr + r, p * RET_HEADS + h))

    tab = pl.BlockSpec((ROWS, RET_D // 2), lambda b, h, r: (r, 0))
    return pl.pallas_call(
        _retention_kernel,
        grid=(batch, RET_HEADS, nr),
        in_specs=[part(0), part(1), part(2), part(3), tab, tab,
                  pl.BlockSpec((None, 1, LANE), lambda b, h, r: (h, 0, 0)),
                  pl.BlockSpec((1, RET_D), lambda b, h, r: (0, h))],
        out_specs=pl.BlockSpec((ROWS, RET_D), lambda b, h, r: (b * nr + r, h)),
        out_shape=jax.ShapeDtypeStruct((m, MIX_W), BF16),
        scratch_shapes=[pltpu.VMEM((RET_D, RET_D), F32)],
        compiler_params=_cparams(("parallel", "parallel", "arbitrary")),
        name="retention",
    )(proj, proj, proj, proj, cos, sin, lg, norm_gain.reshape(1, MIX_W))


GLA_LEVELS = 6


def _gla_consts():
    i, j = _np_masks()
    mats = [(j <= i), (j > i)]
    pair = []
    for lvl in range(GLA_LEVELS):
        s = 32 >> lvl
        blk = (i // (2 * s)) * (2 * s)
        mid = blk + s - 1
        second = (i - blk) >= s
        mats.append(second & (j > mid) & (j <= i))
        mats.append((~second) & (j > i) & (j <= mid))
        jb = (j // (2 * s)) * (2 * s)
        pair.append(second & (jb == blk) & ((j - jb) < s))
    pair.append(i == j)
    em = np.concatenate(mats, axis=0).astype(np.float32)
    pm = np.stack(pair, axis=0).astype(np.float32)
    return jnp.asarray(em, BF16), jnp.asarray(pm, F32)


def _gla_kernel(q_ref, k_ref, v_ref, g_ref, lr_ref, w2_ref, bias_ref, gain_ref, em_ref, pm_ref,
                o_ref, s_ref):
    @pl.when(pl.program_id(1) == 0)
    def _():
        s_ref[...] = jnp.zeros_like(s_ref)

    heads = range(GLA_HEADS)
    ks = [slice(h * GLA_DK, (h + 1) * GLA_DK) for h in heads]
    vs = [slice(h * GLA_DV, (h + 1) * GLA_DV) for h in heads]
    x = _bdot(lr_ref[...], w2_ref[...]) + bias_ref[...]
    la = (jnp.minimum(x, 0.0) - jnp.log1p(jnp.exp(-jnp.abs(x)))) / GLA_TAU
    ex = jnp.exp(_mask_dot(em_ref[...], la))
    e_b = ex[0:CH]
    e_rest = ex[CH:2 * CH]
    q = q_ref[...] * GLA_DK ** -0.5
    k = k_ref[...]
    v = [v_ref[:, s] for s in vs]
    scores = [_bdot_nt(q[:, s], k[:, s]) * pm_ref[GLA_LEVELS] for s in ks]
    for lvl in range(GLA_LEVELS):
        qe = q * ex[(2 + 2 * lvl) * CH:(3 + 2 * lvl) * CH]
        ke = k * ex[(3 + 2 * lvl) * CH:(4 + 2 * lvl) * CH]
        pm = pm_ref[lvl]
        scores = [scores[h] + _bdot_nt(qe[:, ks[h]], ke[:, ks[h]]) * pm for h in heads]
    st = [s_ref[h] for h in heads]
    qb = q * e_b
    kr = k * e_rest
    o = [_bdot(scores[h], v[h]) + _bdot_nt(qb[:, ks[h]], st[h]) for h in heads]
    upd = [_bdot_tn(v[h], kr[:, ks[h]]) for h in heads]
    for h in heads:
        s_ref[h] = st[h] * e_b[CH - 1:CH, ks[h]] + upd[h]
        oh = o[h] * lax.rsqrt(jnp.mean(o[h] * o[h], axis=-1, keepdims=True) + EPS) * gain_ref[:, vs[h]]
        o_ref[:, vs[h]] = (oh * _silu(g_ref[:, vs[h]])).astype(o_ref.dtype)


def _gla(proj, w2p, bias, norm_gain, batch, lp):
    m = proj.shape[0]
    nc = lp // CH
    em, pm = _gla_consts()
    qk_w = GLA_HEADS * GLA_DK
    const = lambda shape: pl.BlockSpec(shape, lambda b, c: (0,) * len(shape))
    return pl.pallas_call(
        _gla_kernel,
        grid=(batch, nc),
        in_specs=[pl.BlockSpec((CH, qk_w), lambda b, c: (b * nc + c, COL_GLA_QK // qk_w)),
                  pl.BlockSpec((CH, qk_w), lambda b, c: (b * nc + c, COL_GLA_QK // qk_w + 1)),
                  pl.BlockSpec((CH, MIX_W), lambda b, c: (b * nc + c, COL_GLA_VG // MIX_W)),
                  pl.BlockSpec((CH, MIX_W), lambda b, c: (b * nc + c, COL_GLA_VG // MIX_W + 1)),
                  pl.BlockSpec((CH, LANE), lambda b, c: (b * nc + c, COL_GLA_LR // LANE)),
                  const((LANE, qk_w)), const((1, qk_w)), const((1, MIX_W)),
                  const(em.shape), const(pm.shape)],
        out_specs=pl.BlockSpec((CH, MIX_W), lambda b, c: (b * nc + c, 0)),
        out_shape=jax.ShapeDtypeStruct((m, MIX_W), BF16),
        scratch_shapes=[pltpu.VMEM((GLA_HEADS, GLA_DV, GLA_DK), F32)],
        compiler_params=_cparams(("parallel", "arbitrary")),
        name="gla",
    )(proj, proj, proj, proj, proj, w2p, bias.reshape(1, -1), norm_gain.reshape(1, MIX_W), em, pm)


def _gdn_kernel(q_ref, k_ref, v_ref, z_ref, ab_ref, cq_ref, ck_ref, cv_ref, alog_ref, dtb_ref,
                gain_ref, tri_ref, blk_ref, o_ref, s_ref, carry_ref):
    @pl.when(pl.program_id(1) == 0)
    def _():
        s_ref[...] = jnp.zeros_like(s_ref)
        carry_ref[...] = jnp.zeros_like(carry_ref)

    def conv_silu(x_ref, c_ref, slot):
        x = x_ref[...]
        ext = jnp.concatenate([carry_ref[slot], x], axis=0)
        carry_ref[slot] = x[CH - 8:, :]
        c = c_ref[...]
        y = c[3:4] * x
        for t in range(1, GDN_CONV):
            y = y + c[3 - t:4 - t] * ext[8 - t:8 - t + CH, :]
        return _silu(y)

    def l2norm(t):
        return t * lax.rsqrt(jnp.sum(t * t, axis=-1, keepdims=True) + EPS)

    qc = conv_silu(q_ref, cq_ref, 0)
    kc = conv_silu(k_ref, ck_ref, 1)
    vc = conv_silu(v_ref, cv_ref, 2)
    ab = ab_ref[...]
    g_t = -jnp.exp(alog_ref[...]) * _softplus(ab + dtb_ref[...])
    beta_t = _sigmoid(ab)
    rep = lambda t, idx: jnp.broadcast_to(t[:, idx:idx + 1], (CH, LANE))
    heads = range(GDN_HEADS)
    hs = [slice(h * GDN_D, (h + 1) * GDN_D) for h in heads]
    gcum_all = _mask_dot(tri_ref[...], jnp.concatenate([rep(g_t, h) for h in heads], axis=1))

    same_blk = blk_ref[...]
    ri = lax.broadcasted_iota(jnp.int32, (CH, CH), 0)
    ci = lax.broadcasted_iota(jnp.int32, (CH, CH), 1)
    eye = (ri == ci).astype(F32)
    q = [l2norm(qc[:, s]) * GDN_D ** -0.5 for s in hs]
    k = [l2norm(kc[:, s]) for s in hs]
    v = [vc[:, s] for s in hs]
    beta = [rep(beta_t, GDN_HEADS + h) for h in heads]
    gcum = [gcum_all[:, s] for s in hs]
    grow = [jnp.transpose(g)[:1, :CH] for g in gcum]
    decay = [jnp.where(ri >= ci, jnp.exp(jnp.minimum(gcum[h][:, :1] - grow[h], 0.0)), 0.0) for h in heads]
    kb = [k[h] * beta[h] for h in heads]
    kq = [_bdot_nt(jnp.concatenate([kb[h], q[h]], axis=0), k[h]) for h in heads]
    a_mat = [jnp.where(ri > ci, kq[h][:CH] * decay[h], 0.0) for h in heads]
    attn = [kq[h][CH:] * decay[h] for h in heads]
    t_inv = _inv_unit_lower(a_mat, same_blk, eye)
    e_g = [jnp.exp(g) for g in gcum]
    sol = [_bdot(t_inv[h], jnp.concatenate([v[h] * beta[h], kb[h] * e_g[h]], axis=1)) for h in heads]
    s = [s_ref[h] for h in heads]
    m = [_bdot(jnp.concatenate([sol[h][:, GDN_D:], q[h] * e_g[h]], axis=0), s[h]) for h in heads]
    v_new = [sol[h][:, :GDN_D] - m[h][:CH] for h in heads]
    o = [m[h][CH:] + _bdot(attn[h], v_new[h]) for h in heads]
    g_last = [g[CH - 1:CH, :] for g in gcum]
    upd = [_bdot_tn(k[h] * jnp.exp(g_last[h] - gcum[h]), v_new[h]) for h in heads]
    gain = gain_ref[...]
    for h in heads:
        s_ref[h] = s[h] * jnp.exp(g_last[h]) + upd[h]
        oh = o[h] * lax.rsqrt(jnp.mean(o[h] * o[h], axis=-1, keepdims=True) + EPS) * gain
        o_ref[:, hs[h]] = (oh * _silu(z_ref[:, hs[h]])).astype(o_ref.dtype)


def _gdn(proj, conv_w, a_log, dt_bias, norm_gain, batch, lp):
    m = proj.shape[0]
    nc = lp // CH
    nh = GDN_HEADS
    pad = lambda t: jnp.pad(t.reshape(1, nh), ((0, 0), (0, LANE - nh)))
    part = lambda p: pl.BlockSpec((CH, MIX_W), lambda b, c: (b * nc + c, COL_GDN // MIX_W + p))
    cpart = lambda p: pl.BlockSpec((GDN_CONV, MIX_W), lambda b, c: (0, p))
    const = lambda shape: pl.BlockSpec(shape, lambda b, c: (0,) * len(shape))
    return pl.pallas_call(
        _gdn_kernel,
        grid=(batch, nc),
        in_specs=[part(0), part(1), part(2), part(3),
                  pl.BlockSpec((CH, LANE), lambda b, c: (b * nc + c, COL_GDN_AB // LANE)),
                  cpart(0), cpart(1), cpart(2),
                  const((1, LANE)), const((1, LANE)), const((1, GDN_D)),
                  const((CH, CH)), const((CH, CH))],
        out_specs=pl.BlockSpec((CH, MIX_W), lambda b, c: (b * nc + c, 0)),
        out_shape=jax.ShapeDtypeStruct((m, MIX_W), BF16),
        scratch_shapes=[pltpu.VMEM((nh, GDN_D, GDN_D), F32), pltpu.VMEM((3, 8, MIX_W), F32)],
        compiler_params=_cparams(("parallel", "arbitrary")),
        name="gdn",
    )(proj, proj, proj, proj, proj, conv_w, conv_w, conv_w, pad(a_log), pad(dt_bias),
      norm_gain.reshape(1, GDN_D), _const_tri(), _const_same_blk())


def _rwkv_prep_kernel(sr_ref, sk_ref, sv_ref, swa_ref, sg_ref, mu_ref, w0_ref, a0_ref, kk_ref, ka_ref,
                      w2_ref, a2_ref, g2_ref, seg_ref, segt_ref,
                      r_ref, lw_ref, k_ref, v_ref, na_ref, bb_ref, g_ref, carry_ref):
    rb = pl.program_id(1)

    @pl.when(rb == 0)
    def _():
        carry_ref[...] = jnp.zeros_like(carry_ref)

    x = jnp.concatenate([sr_ref[...], sk_ref[...], sv_ref[...], swa_ref[...], sg_ref[...]], axis=1)
    ext = jnp.concatenate([carry_ref[...], x], axis=0)
    carry_ref[...] = x[ROWS - 8:, :]
    prev = ext[7:7 + ROWS, :]
    s = x + (prev - x) * mu_ref[...]
    r = s[:, :MIX_W]
    k = s[:, MIX_W:2 * MIX_W]
    v = s[:, 2 * MIX_W:3 * MIX_W]
    wa_src = s[:, RWKV_LR:RWKV_LR + LANE]
    g_src = s[:, RWKV_LR + LANE:]
    lw_ref[...] = -RWKV_DECAY_SCALE * _sigmoid(w0_ref[...] + _bdot(jnp.tanh(wa_src), w2_ref[...]))
    a = _sigmoid(a0_ref[...] + _bdot(wa_src, a2_ref[...]))
    g_ref[...] = _bdot(_sigmoid(g_src), g2_ref[...])
    kk = k * kk_ref[...]
    sq = kk * kk
    sq_hi = sq.astype(BF16)
    sq_lo = (sq - sq_hi.astype(F32)).astype(BF16)
    seg = seg_ref[...]
    hsum = (jnp.dot(sq_hi, seg, preferred_element_type=F32)
            + jnp.dot(sq_lo, seg, preferred_element_type=F32))
    inv = lax.rsqrt(hsum + EPS)
    inv_hi = inv.astype(BF16)
    inv_lo = (inv - inv_hi.astype(F32)).astype(BF16)
    segt = segt_ref[...]
    inv_full = (jnp.dot(inv_hi, segt, preferred_element_type=F32)
                + jnp.dot(inv_lo, segt, preferred_element_type=F32))
    kk = kk * inv_full
    r_ref[...] = r
    k_ref[...] = k * (1.0 + (a - 1.0) * ka_ref[...])
    v_ref[...] = v
    na_ref[...] = -kk
    bb_ref[...] = kk * a


def _rwkv_scan_kernel(r_ref, lw_ref, k_ref, v_ref, na_ref, bb_ref, g_ref, rk_ref, lnw_ref, lnb_ref,
                      cm_ref, blk_ref, o_ref, h_ref):
    @pl.when(pl.program_id(1) == 0)
    def _():
        h_ref[...] = jnp.zeros_like(h_ref)

    n = RWKV_N
    npair = MIX_W // LANE
    ri = lax.broadcasted_iota(jnp.int32, (CH, CH), 0)
    ci = lax.broadcasted_iota(jnp.int32, (CH, CH), 1)
    eye = (ri == ci).astype(F32)
    strict = ri > ci
    ri2 = lax.broadcasted_iota(jnp.int32, (CH, LANE), 0)
    ci2 = lax.broadcasted_iota(jnp.int32, (CH, LANE), 1) % CH
    incl2 = ri2 >= ci2
    lo = lax.broadcasted_iota(jnp.int32, (1, LANE), 1) < n
    vi = lax.broadcasted_iota(jnp.int32, (LANE, LANE), 0) // n
    ki = lax.broadcasted_iota(jnp.int32, (LANE, LANE), 1) // n
    same_head = (vi == ki).astype(F32)
    same_blk = blk_ref[...]

    r, lw, k, v = r_ref[...], lw_ref[...], k_ref[...], v_ref[...]
    na, bb = na_ref[...], bb_ref[...]
    cums = _mask_dot(cm_ref[...], lw)
    cum, rest = cums[:CH], cums[CH:]
    e_cum = jnp.exp(cum)
    e_neg = jnp.exp(-cum)
    e_rest = jnp.exp(rest)
    rt = r * e_cum
    at = na * jnp.exp(cum - lw)
    bt = bb * e_neg
    kt = k * e_neg
    bh = bb * e_rest
    kh = k * e_rest

    pairs = [slice(p * LANE, (p + 1) * LANE) for p in range(npair)]
    pm = []
    for ps in pairs:
        a_p, r_p = at[:, ps], rt[:, ps]
        lhs = jnp.concatenate([jnp.where(lo, a_p, 0.0), jnp.where(lo, r_p, 0.0),
                               jnp.where(lo, 0.0, a_p), jnp.where(lo, 0.0, r_p)], axis=0)
        pm.append(_bdot_nt(lhs, jnp.concatenate([bt[:, ps], kt[:, ps]], axis=0)))
    heads = [(p, s) for p in range(npair) for s in range(2)]
    a_ab = [jnp.where(strict, pm[p][2 * s * CH:(2 * s + 1) * CH, :CH], 0.0) for p, s in heads]
    a_ak = [jnp.where(strict, pm[p][2 * s * CH:(2 * s + 1) * CH, CH:], 0.0) for p, s in heads]
    a_rbk = [jnp.where(incl2, pm[p][(2 * s + 1) * CH:(2 * s + 2) * CH, :], 0.0) for p, s in heads]
    t_inv = _inv_unit_lower([-a for a in a_ab], same_blk, eye)
    akv = [_bdot(a_ak[i], v[:, pairs[p]]) for i, (p, s) in enumerate(heads)]
    sol = [_bdot(t_inv[i], jnp.concatenate([at[:, pairs[p]], akv[i]], axis=1))
           for i, (p, s) in enumerate(heads)]
    sol = [jnp.where(jnp.concatenate([lo, lo], axis=1), sol[2 * p], sol[2 * p + 1]) for p in range(npair)]
    ht = [h_ref[p] for p in range(npair)]
    m = [_bdot_nt(jnp.concatenate([sol[p][:, :LANE], rt[:, pairs[p]]], axis=0), ht[p]) for p in range(npair)]
    uv = [jnp.concatenate([m[p][:CH] + sol[p][:, LANE:], v[:, pairs[p]]], axis=0) for p in range(npair)]
    yy = [_bdot(a_rbk[i], uv[p]) for i, (p, s) in enumerate(heads)]
    upd = [_bdot_tn(uv[p], jnp.concatenate([bh[:, pairs[p]], kh[:, pairs[p]]], axis=0)) for p in range(npair)]
    for p, ps in enumerate(pairs):
        h_ref[p] = ht[p] * e_cum[CH - 1:CH, ps] + upd[p] * same_head
        y = m[p][CH:] + jnp.where(lo, yy[2 * p], yy[2 * p + 1])

        def head_sum(t):
            s0 = jnp.sum(jnp.where(lo, t, 0.0), axis=-1, keepdims=True)
            s1 = jnp.sum(jnp.where(lo, 0.0, t), axis=-1, keepdims=True)
            return jnp.where(lo, s0, s1)

        y = y - head_sum(y) * (1.0 / n)
        y = y * lax.rsqrt(head_sum(y * y) * (1.0 / n) + RWKV_GN_EPS)
        y = y * lnw_ref[:, ps] + lnb_ref[:, ps]
        bonus = head_sum(r[:, ps] * k[:, ps] * rk_ref[:, ps])
        o_ref[:, ps] = ((y + bonus * v[:, ps]) * g_ref[:, ps]).astype(o_ref.dtype)


def _rwkv(proj, mu_p, w0, w2p, a0, a2p, g2p, kk, ka, rk, ln_w, ln_b, batch, lp):
    m = proj.shape[0]
    nr = lp // ROWS
    c = np.arange(MIX_W)[:, None] // RWKV_N == np.arange(LANE)[None, :]
    seg = jnp.asarray(c.astype(np.float32), BF16)
    segt = jnp.asarray(c.T.astype(np.float32), BF16)
    vec = lambda t: t.reshape(1, -1)
    const2 = lambda shape: pl.BlockSpec(shape, lambda b, r: (0, 0))
    row_w = pl.BlockSpec((ROWS, MIX_W), lambda b, r: (b * nr + r, 0))
    wide = jax.ShapeDtypeStruct((m, MIX_W), F32)
    src = lambda width, col: pl.BlockSpec((ROWS, width), lambda b, r: (b * nr + r, col // width))
    r_, lw, k_, v_, na, bb, g = pl.pallas_call(
        _rwkv_prep_kernel,
        grid=(batch, nr),
        in_specs=[src(MIX_W, COL_RWKV), src(MIX_W, COL_RWKV + MIX_W), src(MIX_W, COL_RWKV + 2 * MIX_W),
                  src(LANE, COL_RWKV_WA), src(2 * LANE, COL_RWKV_G),
                  const2((1, RWKV_PW)), const2((1, MIX_W)), const2((1, MIX_W)), const2((1, MIX_W)),
                  const2((1, MIX_W)), const2((LANE, MIX_W)), const2((LANE, MIX_W)),
                  const2((2 * LANE, MIX_W)), const2((MIX_W, LANE)), const2((LANE, MIX_W))],
        out_specs=[row_w] * 7,
        out_shape=[wide] * 7,
        scratch_shapes=[pltpu.VMEM((8, RWKV_PW), F32)],
        compiler_params=_cparams(("parallel", "arbitrary")),
        name="rwkv_prep",
    )(proj, proj, proj, proj, proj, vec(mu_p), vec(w0), vec(a0), vec(kk), vec(ka), w2p, a2p, g2p, seg, segt)

    i, j = _np_masks()
    cm = jnp.asarray(np.concatenate([(j <= i), (j > i)], axis=0).astype(np.float32), BF16)
    nc = lp // CH
    chunk = pl.BlockSpec((CH, MIX_W), lambda b, c: (b * nc + c, 0))
    return pl.pallas_call(
        _rwkv_scan_kernel,
        grid=(batch, nc),
        in_specs=[chunk] * 7 + [const2((1, MIX_W))] * 3 + [const2((2 * CH, CH)), const2((CH, CH))],
        out_specs=chunk,
        out_shape=jax.ShapeDtypeStruct((m, MIX_W), BF16),
        scratch_shapes=[pltpu.VMEM((MIX_W // LANE, LANE, LANE), F32)],
        compiler_params=_cparams(("parallel", "arbitrary")),
        name="rwkv_scan",
    )(r_, lw, k_, v_, na, bb, g, vec(rk), vec(ln_w), vec(ln_b), cm, _const_same_blk())


def _pad_cols(w, width):
    return jnp.pad(w, ((0, 0),) * (w.ndim - 1) + ((0, width - w.shape[-1]),))


def _pad_rows(w, rows, offset=0):
    return jnp.pad(w, ((offset, rows - offset - w.shape[0]), (0, 0)))


def _reorder_w_in(w_in):
    ret_w = 4 * MIX_W
    gla = ret_w
    gla_v = gla + 2 * GLA_HEADS * GLA_DK
    gla_lr = gla_v + 2 * GLA_HEADS * GLA_DV
    gdn = gla_lr + GLA_RANK
    gdn_ab = gdn + 4 * MIX_W
    rwkv = gdn_ab + 2 * GDN_HEADS
    rwkv_wa = rwkv + 3 * MIX_W
    rwkv_g = rwkv_wa + RWKV_W_RANK + RWKV_A_RANK
    w = w_in.astype(BF16)
    pieces = [w[..., :ret_w], w[..., gdn:gdn_ab], w[..., rwkv:rwkv_wa], w[..., gla_v:gla_lr],
              w[..., gla:gla_v], _pad_cols(w[..., gdn_ab:rwkv], LANE), w[..., rwkv_wa:rwkv_g],
              _pad_cols(w[..., rwkv_g:], 2 * LANE), _pad_cols(w[..., gla_lr:gdn], PROJ_W - COL_GLA_LR)]
    out = jnp.concatenate(pieces, axis=-1)
    assert out.shape[-1] == PROJ_W, out.shape
    return out


def _mixer_sublayer(x, h, lyr, big, layer, cos, sin, batch, lp, next_g):
    proj = _matmul(h, big["w_proj"], layer, 1040, PROJ_TN, F32, "proj")

    o_a = _retention(proj, cos, sin, lyr["ret_norm"], batch, lp)
    o_b = _gla(proj, _pad_rows(lyr["gla_w2"], LANE).astype(BF16), lyr["gla_b"], lyr["gla_norm"],
               batch, lp)
    o_c = _gdn(proj, lyr["gdn_conv"], lyr["gdn_a_log"], lyr["gdn_dt_bias"], lyr["gdn_norm"], batch, lp)
    mu_p = jnp.pad(lyr["rwkv_mu"], (0, RWKV_PW - lyr["rwkv_mu"].shape[0]))
    w2p = _pad_rows(lyr["rwkv_w2"], LANE).astype(BF16)
    a2p = _pad_rows(lyr["rwkv_a2"], LANE, RWKV_W_RANK).astype(BF16)
    g2p = _pad_rows(lyr["rwkv_g2"], 2 * LANE).astype(BF16)
    o_d = _rwkv(proj, mu_p, lyr["rwkv_w0"], w2p, lyr["rwkv_a0"], a2p, g2p, lyr["rwkv_kk"],
                lyr["rwkv_ka"], lyr["rwkv_rk"], lyr["rwkv_ln_w"], lyr["rwkv_ln_b"], batch, lp)

    o4 = jnp.stack([o_a, o_b, o_c, o_d], axis=0)
    merged = _gate_merge(h, o4, big["w_gate"], big["w_branch"], layer)
    z = _matmul(merged, big["w_out"], layer, 1040, 1024, F32, "w_out")
    return _norm_residual(z, x, lyr["post_mix"], next_g)


def _ffn_sublayer(x, h, lyr, big, layer, lp, next_g):
    act = _ffn_up(h, big["w_up"], big["ffn_conv"], layer, lp)
    z = _matmul(act, big["w_down"], layer, 520, 512, F32, "ffn_down")
    return _norm_residual(z, x, lyr["post_ffn"], next_g)


_LAYER_KEYS = ("pre_mix", "w_in", "ret_norm", "gla_w2", "gla_b", "gla_norm", "gdn_conv", "gdn_a_log",
               "gdn_dt_bias", "gdn_norm", "rwkv_mu", "rwkv_w0", "rwkv_w2", "rwkv_a0", "rwkv_a2", "rwkv_g2",
               "rwkv_kk", "rwkv_ka", "rwkv_rk", "rwkv_ln_w", "rwkv_ln_b", "w_branch", "w_gate", "w_out",
               "post_mix", "pre_ffn", "w_up", "ffn_conv", "w_down", "post_ffn")
_BIG_KEYS = ("w_in", "w_gate", "w_branch", "w_out", "w_up", "w_down", "ffn_conv")


def _big_weights(params):
    big = {k: params[k].astype(BF16) for k in ("w_gate", "w_branch", "w_out", "w_up", "w_down")}
    big["w_proj"] = _reorder_w_in(params["w_in"])
    big["ffn_conv"] = params["ffn_conv"]
    return big


def _trunk(x, meta, params):
    batch, seq, d = x.shape
    depth = params["pre_mix"].shape[0]
    l = N_META + seq
    lp = -(-l // ROWS) * ROWS
    hcat = jnp.concatenate([jnp.broadcast_to(meta.astype(x.dtype)[None], (batch, N_META, d)), x,
                            jnp.zeros((batch, lp - l, d), x.dtype)], axis=1)
    xr = hcat.reshape(batch * lp, d)

    half = RET_D // 2
    inv_freq = ROPE_BASE ** (-jnp.arange(half, dtype=F32) / half)
    ang = jnp.arange(lp, dtype=F32)[:, None] * inv_freq[None, :]
    cos, sin = jnp.cos(ang), jnp.sin(ang)

    big = _big_weights(params)
    layers = [{k: params[k][i] for k in _LAYER_KEYS if k not in _BIG_KEYS} for i in range(depth)]
    h = _prenorm(xr, layers[0]["pre_mix"])
    for i, lyr in enumerate(layers):
        xr, h = _mixer_sublayer(xr, h, lyr, big, i, cos, sin, batch, lp, lyr["pre_ffn"])
        next_g = layers[i + 1]["pre_mix"] if i + 1 < depth else lyr["pre_ffn"]
        xr, h = _ffn_sublayer(xr, h, lyr, big, i, lp, next_g)
    return xr.reshape(batch, lp, d)[:, N_META:l]


def kernel(x, meta, pre_mix, w_in, ret_norm, gla_w2, gla_b, gla_norm, gdn_conv, gdn_a_log, gdn_dt_bias,
           gdn_norm, rwkv_mu, rwkv_w0, rwkv_w2, rwkv_a0, rwkv_a2, rwkv_g2, rwkv_kk, rwkv_ka, rwkv_rk,
           rwkv_ln_w, rwkv_ln_b, w_branch, w_gate, w_out, post_mix, pre_ffn, w_up, ffn_conv, w_down, post_ffn):
    params = dict(zip(_LAYER_KEYS, (pre_mix, w_in, ret_norm, gla_w2, gla_b, gla_norm, gdn_conv, gdn_a_log,
                                    gdn_dt_bias, gdn_norm, rwkv_mu, rwkv_w0, rwkv_w2, rwkv_a0, rwkv_a2,
                                    rwkv_g2, rwkv_kk, rwkv_ka, rwkv_rk, rwkv_ln_w, rwkv_ln_b, w_branch,
                                    w_gate, w_out, post_mix, pre_ffn, w_up, ffn_conv, w_down, post_ffn)))
    return _trunk(x, meta, params)
```

```python
import functools
import math

import numpy as np
import jax
import jax.numpy as jnp
from jax import lax
from jax.experimental import pallas as pl
from jax.experimental.pallas import tpu as pltpu

F32 = jnp.float32
BF16 = jnp.bfloat16

D_MODEL = 4096
N_META = 16
N_BRANCH = 4
MIX_W = D_MODEL // N_BRANCH
EPS = 1e-6
D_FF = 11008
FFN_CONV = 3

RET_HEADS, RET_D = 4, 256
ROPE_BASE = 10000.0
GLA_HEADS, GLA_DK, GLA_DV, GLA_RANK, GLA_TAU = 4, 128, 256, 16, 16.0
GDN_HEADS, GDN_D, GDN_CONV = 8, 128, 4
RWKV_HEADS, RWKV_N = 16, 64
RWKV_W_RANK, RWKV_A_RANK, RWKV_G_RANK = 64, 64, 160
RWKV_DECAY_SCALE = 0.606531
RWKV_GN_EPS = 64e-5

LANE = 128
CH = 64
SUB = 16
ROWS = 320
VMEM_LIMIT = 60 * 1024 * 1024

RWKV_LR = 3 * MIX_W
RWKV_PW = 3 * MIX_W + LANE + 2 * LANE
COL_RET = 0
COL_GDN = COL_RET + 4 * MIX_W
COL_RWKV = COL_GDN + 4 * MIX_W
COL_GLA_VG = COL_RWKV + 3 * MIX_W
COL_GLA_QK = COL_GLA_VG + 2 * MIX_W
COL_GDN_AB = COL_GLA_QK + 2 * GLA_HEADS * GLA_DK
COL_RWKV_WA = COL_GDN_AB + LANE
COL_RWKV_G = COL_RWKV_WA + LANE
COL_GLA_LR = COL_RWKV_G + 2 * LANE
PROJ_TN = 1024
PROJ_W = -(-(COL_GLA_LR + LANE) // PROJ_TN) * PROJ_TN


def _row_tile(m, pref):
    best = 16
    for t in range(16, min(m, pref) + 1, 16):
        if m % t == 0:
            best = t
    assert m % best == 0, (m, pref)
    return best


def _cparams(sem):
    return pltpu.CompilerParams(dimension_semantics=sem, vmem_limit_bytes=VMEM_LIMIT)


def _bdot(a, b):
    return jnp.dot(a.astype(BF16), b.astype(BF16), preferred_element_type=F32)


def _bdot_nt(a, b):
    return lax.dot_general(a.astype(BF16), b.astype(BF16), (((1,), (1,)), ((), ())),
                           preferred_element_type=F32)


def _bdot_tn(a, b):
    return lax.dot_general(a.astype(BF16), b.astype(BF16), (((0,), (0,)), ((), ())),
                           preferred_element_type=F32)


def _mask_dot(mask, x):
    hi = x.astype(BF16)
    lo = (x - hi.astype(F32)).astype(BF16)
    return (jnp.dot(mask, hi, preferred_element_type=F32)
            + jnp.dot(mask, lo, preferred_element_type=F32))


def _sigmoid(x):
    return 0.5 + 0.5 * jnp.tanh(0.5 * x)


def _silu(x):
    return x * _sigmoid(x)


def _softplus(x):
    return jnp.maximum(x, 0.0) + jnp.log1p(jnp.exp(-jnp.abs(x)))


def _inv_unit_lower(mats, same_blk, eye):
    each = lambda f, *ls: [f(*t) for t in zip(*ls)]
    d = [a * same_blk for a in mats]
    l = each(lambda a, dd: a - dd, mats, d)
    d2 = each(_bdot, d, d)
    d4 = each(_bdot, d2, d2)
    d8 = each(_bdot, d4, d4)
    p = [eye - dd for dd in d]
    p = each(lambda pp, m: pp + _bdot(pp, m), p, d2)
    p = each(lambda pp, m: pp + _bdot(pp, m), p, d4)
    td = each(lambda pp, m: pp + _bdot(pp, m), p, d8)
    x = each(_bdot, td, l)
    x2 = each(_bdot, x, x)
    q = [eye - xx for xx in x]
    q = each(lambda qq, m: qq + _bdot(qq, m), q, x2)
    return each(_bdot, q, td)


def _np_masks():
    i = np.arange(CH)[:, None]
    j = np.arange(CH)[None, :]
    return i, j


def _const_tri():
    i, j = _np_masks()
    return jnp.asarray((j <= i).astype(np.float32), BF16)


def _const_same_blk():
    i, j = _np_masks()
    return jnp.asarray((i // SUB == j // SUB).astype(np.float32), F32)


def _prenorm_kernel(x_ref, g_ref, o_ref):
    x = x_ref[...]
    y = x * lax.rsqrt(jnp.mean(x * x, axis=-1, keepdims=True) + EPS)
    o_ref[...] = (y * g_ref[...]).astype(o_ref.dtype)


def _prenorm(x, g):
    m, d = x.shape
    tm = _row_tile(m, 416)
    return pl.pallas_call(
        _prenorm_kernel,
        grid=(m // tm,),
        in_specs=[pl.BlockSpec((tm, d), lambda i: (i, 0)),
                  pl.BlockSpec((1, d), lambda i: (0, 0))],
        out_specs=pl.BlockSpec((tm, d), lambda i: (i, 0)),
        out_shape=jax.ShapeDtypeStruct((m, d), BF16),
        compiler_params=_cparams(("parallel",)),
        name="prenorm",
    )(x, g.reshape(1, d))


def _norm_residual_kernel(z_ref, x_ref, pg_ref, ng_ref, xo_ref, ho_ref):
    z = z_ref[...]
    y = z * lax.rsqrt(jnp.mean(z * z, axis=-1, keepdims=True) + EPS) * pg_ref[...]
    xn = x_ref[...] + y
    xo_ref[...] = xn
    h = xn * lax.rsqrt(jnp.mean(xn * xn, axis=-1, keepdims=True) + EPS) * ng_ref[...]
    ho_ref[...] = h.astype(ho_ref.dtype)


def _norm_residual(z, x, post_g, next_g):
    m, d = x.shape
    tm = _row_tile(m, 208)
    row = pl.BlockSpec((tm, d), lambda i: (i, 0))
    vec = pl.BlockSpec((1, d), lambda i: (0, 0))
    return pl.pallas_call(
        _norm_residual_kernel,
        grid=(m // tm,),
        in_specs=[row, row, vec, vec],
        out_specs=[row, row],
        out_shape=[jax.ShapeDtypeStruct((m, d), F32), jax.ShapeDtypeStruct((m, d), BF16)],
        compiler_params=_cparams(("parallel",)),
        name="norm_residual",
    )(z, x, post_g.reshape(1, d), next_g.reshape(1, d))


class _SideCast:
    def __init__(self, src, layer, steps):
        self.r, self.c = src.shape[1:]
        best = None
        for split in (1, 2, 4):
            if self.c % (split * LANE):
                continue
            for br in range(16, self.r + 1, 16):
                nblk = (self.r // br) * split
                if self.r % br == 0 and nblk <= steps and (best is None or nblk > best[0]):
                    best = (nblk, br, self.c // split)
        assert best is not None, (src.shape, steps)
        self.nblk, self.br, self.bc = best
        self.ncol = self.c // self.bc
        self.src, self.layer = src, layer

    def _block(self, step):
        b = jnp.minimum(step, self.nblk - 1)
        return b // self.ncol, b % self.ncol

    def in_spec(self, step_of):
        return pl.BlockSpec((None, self.br, self.bc), lambda *g: (self.layer,) + self._block(step_of(*g)))

    def out_spec(self, step_of):
        return pl.BlockSpec((self.br, self.bc), lambda *g: self._block(step_of(*g)))

    def out_shape(self):
        return jax.ShapeDtypeStruct((self.r, self.c), BF16)


def _side_cast(step, nblk, src_ref, dst_ref):
    @pl.when(step < nblk)
    def _():
        dst_ref[...] = src_ref[...].astype(dst_ref.dtype)


def _matmul_kernel(x_ref, w_ref, o_ref):
    o_ref[...] = jnp.dot(x_ref[...], w_ref[...], preferred_element_type=F32).astype(o_ref.dtype)


def _matmul_cast_kernel(x_ref, w_ref, src_ref, o_ref, dst_ref, *, nj, nblk):
    o_ref[...] = jnp.dot(x_ref[...], w_ref[...], preferred_element_type=F32).astype(o_ref.dtype)
    _side_cast(pl.program_id(0) * nj + pl.program_id(1), nblk, src_ref, dst_ref)


def _matmul(x, w, layer, tm, tn, out_dtype, name, cast=None):
    m, k = x.shape
    n = w.shape[-1]
    tm = _row_tile(m, tm)
    assert n % tn == 0, (n, tn)
    nj = n // tn
    w_spec = (pl.BlockSpec((k, tn), lambda i, j: (0, j)) if layer is None
              else pl.BlockSpec((None, k, tn), lambda i, j: (layer, 0, j)))
    x_spec = pl.BlockSpec((tm, k), lambda i, j: (i, 0))
    o_spec = pl.BlockSpec((tm, tn), lambda i, j: (i, j))
    o_shape = jax.ShapeDtypeStruct((m, n), out_dtype)
    if cast is None:
        return pl.pallas_call(
            _matmul_kernel, grid=(m // tm, nj), in_specs=[x_spec, w_spec], out_specs=o_spec,
            out_shape=o_shape, compiler_params=_cparams(("parallel", "parallel")), name=name,
        )(x, w)
    side = _SideCast(cast[0], cast[1], (m // tm) * nj)
    step_of = lambda i, j: i * nj + j
    x_spec = pl.BlockSpec((tm, k), lambda i, j: (i, 0), pipeline_mode=pl.Buffered(1))
    return pl.pallas_call(
        functools.partial(_matmul_cast_kernel, nj=nj, nblk=side.nblk),
        grid=(m // tm, nj),
        in_specs=[x_spec, w_spec, side.in_spec(step_of)],
        out_specs=[o_spec, side.out_spec(step_of)],
        out_shape=[o_shape, side.out_shape()],
        compiler_params=_cparams(("arbitrary", "arbitrary")),
        name=name,
    )(x, w, side.src)


def _gate_merge_kernel(h_ref, o_ref, wg_ref, wb_ref, src_ref, out_ref, dst_ref, acc_ref, *, nj, nblk):
    n = pl.program_id(2)
    gate = _sigmoid(jnp.dot(h_ref[...], wg_ref[...], preferred_element_type=F32))
    y = jnp.dot(o_ref[...], wb_ref[...], preferred_element_type=F32)

    @pl.when(n == 0)
    def _():
        acc_ref[...] = gate * y

    @pl.when(n > 0)
    def _():
        acc_ref[...] += gate * y

    @pl.when(n == N_BRANCH - 1)
    def _():
        out_ref[...] = acc_ref[...].astype(out_ref.dtype)

    _side_cast((pl.program_id(0) * nj + pl.program_id(1)) * N_BRANCH + n, nblk, src_ref, dst_ref)


def _gate_merge(h, o4, w_gate, w_branch, cast, tn=1024):
    m, d = h.shape
    tm = _row_tile(m, 1040)
    nj = d // tn
    layer = cast[1]
    side = _SideCast(cast[0], layer, (m // tm) * nj * N_BRANCH)
    step_of = lambda i, j, n: (i * nj + j) * N_BRANCH + n
    return pl.pallas_call(
        functools.partial(_gate_merge_kernel, nj=nj, nblk=side.nblk),
        grid=(m // tm, nj, N_BRANCH),
        in_specs=[pl.BlockSpec((tm, d), lambda i, j, n: (i, 0), pipeline_mode=pl.Buffered(1)),
                  pl.BlockSpec((None, tm, MIX_W), lambda i, j, n: (n, i, 0)),
                  pl.BlockSpec((d, tn), lambda i, j, n: (0, n * nj + j)),
                  pl.BlockSpec((None, None, MIX_W, tn), lambda i, j, n: (layer, n, 0, j)),
                  side.in_spec(step_of)],
        out_specs=[pl.BlockSpec((tm, tn), lambda i, j, n: (i, j)), side.out_spec(step_of)],
        out_shape=[jax.ShapeDtypeStruct((m, d), BF16), side.out_shape()],
        scratch_shapes=[pltpu.VMEM((tm, tn), F32)],
        compiler_params=_cparams(("arbitrary", "arbitrary", "arbitrary")),
        name="gate_merge",
    )(h, o4, w_gate, w_branch, side.src)


def _ffn_up_kernel(h_ref, halo_ref, wa_ref, wb_ref, ca_ref, cb_ref, src_ref, o_ref, dst_ref,
                   *, tiles_per_seq, nj, nblk):
    i = pl.program_id(0)
    first = (i % tiles_per_seq) == 0
    h = h_ref[...]
    halo = halo_ref[...]
    row8 = lax.broadcasted_iota(jnp.int32, (8, 1), 0)

    def conv(w_ref, c_ref):
        u = jnp.dot(h, w_ref[...], preferred_element_type=F32)
        uh = jnp.dot(halo, w_ref[...], preferred_element_type=F32)
        uh = jnp.where(first, 0.0, uh)
        p1, p2 = uh[15:16], uh[14:15]
        u1 = pltpu.roll(u, 1, 0)
        u2 = pltpu.roll(u, 2, 0)
        c = c_ref[...]
        y = c[2:3] * u + c[1:2] * u1 + c[0:1] * u2
        f1 = jnp.where(row8 == 0, p1, u1[:8])
        f2 = jnp.where(row8 == 0, p2, jnp.where(row8 == 1, p1, u2[:8]))
        top = c[2:3] * u[:8] + c[1:2] * f1 + c[0:1] * f2
        return jnp.concatenate([top, y[8:]], axis=0)

    a = _silu(conv(wa_ref, ca_ref))
    b = conv(wb_ref, cb_ref)
    o_ref[...] = (a * b).astype(o_ref.dtype)
    _side_cast(i * nj + pl.program_id(1), nblk, src_ref, dst_ref)


def _ffn_up(h, w_up, conv_w, lp, cast, tn=256):
    m, d = h.shape
    tm = _row_tile(lp, 2080)
    nj = D_FF // tn
    halo_blk = tm // 16
    side = _SideCast(cast[0], cast[1], (m // tm) * nj)
    step_of = lambda i, j: i * nj + j
    layer = cast[1]
    kern = functools.partial(_ffn_up_kernel, tiles_per_seq=lp // tm, nj=nj, nblk=side.nblk)
    return pl.pallas_call(
        kern,
        grid=(m // tm, nj),
        in_specs=[pl.BlockSpec((tm, d), lambda i, j: (i, 0), pipeline_mode=pl.Buffered(1)),
                  pl.BlockSpec((16, d), lambda i, j: (jnp.maximum(i * halo_blk - 1, 0), 0)),
                  pl.BlockSpec((d, tn), lambda i, j: (0, j)),
                  pl.BlockSpec((d, tn), lambda i, j: (0, nj + j)),
                  pl.BlockSpec((None, FFN_CONV, tn), lambda i, j: (layer, 0, j)),
                  pl.BlockSpec((None, FFN_CONV, tn), lambda i, j: (layer, 0, nj + j)),
                  side.in_spec(step_of)],
        out_specs=[pl.BlockSpec((tm, tn), lambda i, j: (i, j)), side.out_spec(step_of)],
        out_shape=[jax.ShapeDtypeStruct((m, D_FF), BF16), side.out_shape()],
        compiler_params=_cparams(("arbitrary", "arbitrary")),
        name="ffn_up",
    )(h, h, w_up, w_up, conv_w, conv_w, side.src)


def _retention_kernel(q_ref, k_ref, v_ref, g_ref, cos_ref, sin_ref, lg_ref, gain_ref, o_ref, s_ref):
    r = pl.program_id(2)

    @pl.when(r == 0)
    def _():
        s_ref[...] = jnp.zeros_like(s_ref)

    half = RET_D // 2
    cos, sin = cos_ref[...], sin_ref[...]

    def rope(t):
        t1, t2 = t[:, :half], t[:, half:]
        return jnp.concatenate([t1 * cos - t2 * sin, t1 * sin + t2 * cos], axis=1)

    q = rope(q_ref[...]) * RET_D ** -0.5
    k = rope(k_ref[...])
    v = v_ref[...]
    n = q.shape[0]
    lg = lg_ref[...][:, :1]
    ri = lax.broadcasted_iota(jnp.int32, (n, n), 0)
    ci = lax.broadcasted_iota(jnp.int32, (n, n), 1)
    rel = (ri - ci).astype(F32)
    decay = jnp.where(rel >= 0, jnp.exp(lg * jnp.maximum(rel, 0.0)), 0.0)
    idx = lax.broadcasted_iota(jnp.int32, (n, 1), 0).astype(F32)
    q_decay = jnp.exp(lg * (idx + 1.0))
    k_decay = jnp.exp(lg * (n - 1.0 - idx))
    s = s_ref[...]
    scores = _bdot_nt(q, k) * decay
    o = _bdot(scores, v) + _bdot(q, s) * q_decay
    s_ref[...] = s * jnp.exp(lg * float(n)) + _bdot_tn(k * k_decay, v)
    o = o - jnp.mean(o, axis=-1, keepdims=True)
    o = o * lax.rsqrt(jnp.mean(o * o, axis=-1, keepdims=True) + EPS) * gain_ref[...]
    o_ref[...] = (o * _silu(g_ref[...])).astype(o_ref.dtype)


def _retention(proj, cos, sin, norm_gain, batch, lp):
    m = proj.shape[0]
    nr = lp // ROWS
    gamma = 1.0 - 2.0 ** (-5.0 - np.arange(RET_HEADS, dtype=np.float64))
    lg = jnp.asarray(np.broadcast_to(np.log(gamma)[:, None, None], (RET_HEADS, 1, LANE)), F32)

    def part(p):
        return pl.BlockSpec((ROWS, RET_D), lambda b, h, r: (b * nr + r, p * RET_HEADS + h))

    tab = pl.BlockSpec((ROWS, RET_D // 2), lambda b, h, r: (r, 0))
    return pl.pallas_call(
        _retention_kernel,
        grid=(batch, RET_HEADS, nr),
        in_specs=[part(0), part(1), part(2), part(3), tab, tab,
                  pl.BlockSpec((None, 1, LANE), lambda b, h, r: (h, 0, 0)),
                  pl.BlockSpec((1, RET_D), lambda b, h, r: (0, h))],
        out_specs=pl.BlockSpec((ROWS, RET_D), lambda b, h, r: (b * nr + r, h)),
        out_shape=jax.ShapeDtypeStruct((m, MIX_W), BF16),
        scratch_shapes=[pltpu.VMEM((RET_D, RET_D), F32)],
        compiler_params=_cparams(("parallel", "parallel", "arbitrary")),
        name="retention",
    )(proj, proj, proj, proj, cos, sin, lg, norm_gain.reshape(1, MIX_W))


GLA_LEVELS = 6


def _gla_consts():
    i, j = _np_masks()
    mats = [(j <= i), (j > i)]
    pair = []
    for lvl in range(GLA_LEVELS):
        s = 32 >> lvl
        blk = (i // (2 * s)) * (2 * s)
        mid = blk + s - 1
        second = (i - blk) >= s
        mats.append(second & (j > mid) & (j <= i))
        mats.append((~second) & (j > i) & (j <= mid))
        jb = (j // (2 * s)) * (2 * s)
        pair.append(second & (jb == blk) & ((j - jb) < s))
    pair.append(i == j)
    em = np.concatenate(mats, axis=0).astype(np.float32)
    pm = np.stack(pair, axis=0).astype(np.float32)
    return jnp.asarray(em, BF16), jnp.asarray(pm, F32)


def _gla_kernel(q_ref, k_ref, v_ref, g_ref, lr_ref, w2_ref, bias_ref, gain_ref, em_ref, pm_ref,
                o_ref, s_ref):
    @pl.when(pl.program_id(1) == 0)
    def _():
        s_ref[...] = jnp.zeros_like(s_ref)

    heads = range(GLA_HEADS)
    ks = [slice(h * GLA_DK, (h + 1) * GLA_DK) for h in heads]
    vs = [slice(h * GLA_DV, (h + 1) * GLA_DV) for h in heads]
    x = _bdot(lr_ref[...], w2_ref[...]) + bias_ref[...]
    la = (jnp.minimum(x, 0.0) - jnp.log1p(jnp.exp(-jnp.abs(x)))) / GLA_TAU
    ex = jnp.exp(_mask_dot(em_ref[...], la))
    e_b = ex[0:CH]
    e_rest = ex[CH:2 * CH]
    q = q_ref[...] * GLA_DK ** -0.5
    k = k_ref[...]
    v = [v_ref[:, s] for s in vs]
    scores = [_bdot_nt(q[:, s], k[:, s]) * pm_ref[GLA_LEVELS] for s in ks]
    for lvl in range(GLA_LEVELS):
        qe = q * ex[(2 + 2 * lvl) * CH:(3 + 2 * lvl) * CH]
        ke = k * ex[(3 + 2 * lvl) * CH:(4 + 2 * lvl) * CH]
        pm = pm_ref[lvl]
        scores = [scores[h] + _bdot_nt(qe[:, ks[h]], ke[:, ks[h]]) * pm for h in heads]
    st = [s_ref[h] for h in heads]
    qb = q * e_b
    kr = k * e_rest
    o = [_bdot(scores[h], v[h]) + _bdot_nt(qb[:, ks[h]], st[h]) for h in heads]
    upd = [_bdot_tn(v[h], kr[:, ks[h]]) for h in heads]
    for h in heads:
        s_ref[h] = st[h] * e_b[CH - 1:CH, ks[h]] + upd[h]
        oh = o[h] * lax.rsqrt(jnp.mean(o[h] * o[h], axis=-1, keepdims=True) + EPS) * gain_ref[:, vs[h]]
        o_ref[:, vs[h]] = (oh * _silu(g_ref[:, vs[h]])).astype(o_ref.dtype)


def _gla(proj, w2p, bias, norm_gain, batch, lp):
    m = proj.shape[0]
    nc = lp // CH
    em, pm = _gla_consts()
    qk_w = GLA_HEADS * GLA_DK
    const = lambda shape: pl.BlockSpec(shape, lambda b, c: (0,) * len(shape))
    return pl.pallas_call(
        _gla_kernel,
        grid=(batch, nc),
        in_specs=[pl.BlockSpec((CH, qk_w), lambda b, c: (b * nc + c, COL_GLA_QK // qk_w)),
                  pl.BlockSpec((CH, qk_w), lambda b, c: (b * nc + c, COL_GLA_QK // qk_w + 1)),
                  pl.BlockSpec((CH, MIX_W), lambda b, c: (b * nc + c, COL_GLA_VG // MIX_W)),
                  pl.BlockSpec((CH, MIX_W), lambda b, c: (b * nc + c, COL_GLA_VG // MIX_W + 1)),
                  pl.BlockSpec((CH, LANE), lambda b, c: (b * nc + c, COL_GLA_LR // LANE)),
                  const((LANE, qk_w)), const((1, qk_w)), const((1, MIX_W)),
                  const(em.shape), const(pm.shape)],
        out_specs=pl.BlockSpec((CH, MIX_W), lambda b, c: (b * nc + c, 0)),
        out_shape=jax.ShapeDtypeStruct((m, MIX_W), BF16),
        scratch_shapes=[pltpu.VMEM((GLA_HEADS, GLA_DV, GLA_DK), F32)],
        compiler_params=_cparams(("parallel", "arbitrary")),
        name="gla",
    )(proj, proj, proj, proj, proj, w2p, bias.reshape(1, -1), norm_gain.reshape(1, MIX_W), em, pm)


def _gdn_kernel(q_ref, k_ref, v_ref, z_ref, ab_ref, cq_ref, ck_ref, cv_ref, alog_ref, dtb_ref,
                gain_ref, tri_ref, blk_ref, o_ref, s_ref, carry_ref):
    @pl.when(pl.program_id(1) == 0)
    def _():
        s_ref[...] = jnp.zeros_like(s_ref)
        carry_ref[...] = jnp.zeros_like(carry_ref)

    def conv_silu(x_ref, c_ref, slot):
        x = x_ref[...]
        ext = jnp.concatenate([carry_ref[slot], x], axis=0)
        carry_ref[slot] = x[CH - 8:, :]
        c = c_ref[...]
        y = c[3:4] * x
        for t in range(1, GDN_CONV):
            y = y + c[3 - t:4 - t] * ext[8 - t:8 - t + CH, :]
        return _silu(y)

    def l2norm(t):
        return t * lax.rsqrt(jnp.sum(t * t, axis=-1, keepdims=True) + EPS)

    qc = conv_silu(q_ref, cq_ref, 0)
    kc = conv_silu(k_ref, ck_ref, 1)
    vc = conv_silu(v_ref, cv_ref, 2)
    ab = ab_ref[...]
    g_t = -jnp.exp(alog_ref[...]) * _softplus(ab + dtb_ref[...])
    beta_t = _sigmoid(ab)
    rep = lambda t, idx: jnp.broadcast_to(t[:, idx:idx + 1], (CH, LANE))
    heads = range(GDN_HEADS)
    hs = [slice(h * GDN_D, (h + 1) * GDN_D) for h in heads]
    gcum_all = _mask_dot(tri_ref[...], jnp.concatenate([rep(g_t, h) for h in heads], axis=1))

    same_blk = blk_ref[...]
    ri = lax.broadcasted_iota(jnp.int32, (CH, CH), 0)
    ci = lax.broadcasted_iota(jnp.int32, (CH, CH), 1)
    eye = (ri == ci).astype(F32)
    q = [l2norm(qc[:, s]) * GDN_D ** -0.5 for s in hs]
    k = [l2norm(kc[:, s]) for s in hs]
    v = [vc[:, s] for s in hs]
    beta = [rep(beta_t, GDN_HEADS + h) for h in heads]
    gcum = [gcum_all[:, s] for s in hs]
    grow = [jnp.transpose(g)[:1, :CH] for g in gcum]
    decay = [jnp.where(ri >= ci, jnp.exp(jnp.minimum(gcum[h][:, :1] - grow[h], 0.0)), 0.0) for h in heads]
    kb = [k[h] * beta[h] for h in heads]
    kq = [_bdot_nt(jnp.concatenate([kb[h], q[h]], axis=0), k[h]) for h in heads]
    a_mat = [jnp.where(ri > ci, kq[h][:CH] * decay[h], 0.0) for h in heads]
    attn = [kq[h][CH:] * decay[h] for h in heads]
    t_inv = _inv_unit_lower(a_mat, same_blk, eye)
    e_g = [jnp.exp(g) for g in gcum]
    sol = [_bdot(t_inv[h], jnp.concatenate([v[h] * beta[h], kb[h] * e_g[h]], axis=1)) for h in heads]
    s = [s_ref[h] for h in heads]
    m = [_bdot(jnp.concatenate([sol[h][:, GDN_D:], q[h] * e_g[h]], axis=0), s[h]) for h in heads]
    v_new = [sol[h][:, :GDN_D] - m[h][:CH] for h in heads]
    o = [m[h][CH:] + _bdot(attn[h], v_new[h]) for h in heads]
    g_last = [g[CH - 1:CH, :] for g in gcum]
    upd = [_bdot_tn(k[h] * jnp.exp(g_last[h] - gcum[h]), v_new[h]) for h in heads]
    gain = gain_ref[...]
    for h in heads:
        s_ref[h] = s[h] * jnp.exp(g_last[h]) + upd[h]
        oh = o[h] * lax.rsqrt(jnp.mean(o[h] * o[h], axis=-1, keepdims=True) + EPS) * gain
        o_ref[:, hs[h]] = (oh * _silu(z_ref[:, hs[h]])).astype(o_ref.dtype)


def _gdn(proj, conv_w, a_log, dt_bias, norm_gain, batch, lp):
    m = proj.shape[0]
    nc = lp // CH
    nh = GDN_HEADS
    pad = lambda t: jnp.pad(t.reshape(1, nh), ((0, 0), (0, LANE - nh)))
    part = lambda p: pl.BlockSpec((CH, MIX_W), lambda b, c: (b * nc + c, COL_GDN // MIX_W + p))
    cpart = lambda p: pl.BlockSpec((GDN_CONV, MIX_W), lambda b, c: (0, p))
    const = lambda shape: pl.BlockSpec(shape, lambda b, c: (0,) * len(shape))
    return pl.pallas_call(
        _gdn_kernel,
        grid=(batch, nc),
        in_specs=[part(0), part(1), part(2), part(3),
                  pl.BlockSpec((CH, LANE), lambda b, c: (b * nc + c, COL_GDN_AB // LANE)),
                  cpart(0), cpart(1), cpart(2),
                  const((1, LANE)), const((1, LANE)), const((1, GDN_D)),
                  const((CH, CH)), const((CH, CH))],
        out_specs=pl.BlockSpec((CH, MIX_W), lambda b, c: (b * nc + c, 0)),
        out_shape=jax.ShapeDtypeStruct((m, MIX_W), BF16),
        scratch_shapes=[pltpu.VMEM((nh, GDN_D, GDN_D), F32), pltpu.VMEM((3, 8, MIX_W), F32)],
        compiler_params=_cparams(("parallel", "arbitrary")),
        name="gdn",
    )(proj, proj, proj, proj, proj, conv_w, conv_w, conv_w, pad(a_log), pad(dt_bias),
      norm_gain.reshape(1, GDN_D), _const_tri(), _const_same_blk())


def _rwkv_prep_kernel(sr_ref, sk_ref, sv_ref, swa_ref, sg_ref, mu_ref, w0_ref, a0_ref, kk_ref, ka_ref,
                      w2_ref, a2_ref, g2_ref, seg_ref, segt_ref,
                      r_ref, lw_ref, k_ref, v_ref, na_ref, bb_ref, g_ref, carry_ref):
    rb = pl.program_id(1)

    @pl.when(rb == 0)
    def _():
        carry_ref[...] = jnp.zeros_like(carry_ref)

    x = jnp.concatenate([sr_ref[...], sk_ref[...], sv_ref[...], swa_ref[...], sg_ref[...]], axis=1)
    ext = jnp.concatenate([carry_ref[...], x], axis=0)
    carry_ref[...] = x[ROWS - 8:, :]
    prev = ext[7:7 + ROWS, :]
    s = x + (prev - x) * mu_ref[...]
    r = s[:, :MIX_W]
    k = s[:, MIX_W:2 * MIX_W]
    v = s[:, 2 * MIX_W:3 * MIX_W]
    wa_src = s[:, RWKV_LR:RWKV_LR + LANE]
    g_src = s[:, RWKV_LR + LANE:]
    lw_ref[...] = -RWKV_DECAY_SCALE * _sigmoid(w0_ref[...] + _bdot(jnp.tanh(wa_src), w2_ref[...]))
    a = _sigmoid(a0_ref[...] + _bdot(wa_src, a2_ref[...]))
    g_ref[...] = _bdot(_sigmoid(g_src), g2_ref[...])
    kk = k * kk_ref[...]
    sq = kk * kk
    sq_hi = sq.astype(BF16)
    sq_lo = (sq - sq_hi.astype(F32)).astype(BF16)
    seg = seg_ref[...]
    hsum = (jnp.dot(sq_hi, seg, preferred_element_type=F32)
            + jnp.dot(sq_lo, seg, preferred_element_type=F32))
    inv = lax.rsqrt(hsum + EPS)
    inv_hi = inv.astype(BF16)
    inv_lo = (inv - inv_hi.astype(F32)).astype(BF16)
    segt = segt_ref[...]
    inv_full = (jnp.dot(inv_hi, segt, preferred_element_type=F32)
                + jnp.dot(inv_lo, segt, preferred_element_type=F32))
    kk = kk * inv_full
    r_ref[...] = r
    k_ref[...] = k * (1.0 + (a - 1.0) * ka_ref[...])
    v_ref[...] = v
    na_ref[...] = -kk
    bb_ref[...] = kk * a


def _rwkv_scan_kernel(r_ref, lw_ref, k_ref, v_ref, na_ref, bb_ref, g_ref, rk_ref, lnw_ref, lnb_ref,
                      cm_ref, blk_ref, o_ref, h_ref):
    @pl.when(pl.program_id(1) == 0)
    def _():
        h_ref[...] = jnp.zeros_like(h_ref)

    n = RWKV_N
    npair = MIX_W // LANE
    ri = lax.broadcasted_iota(jnp.int32, (CH, CH), 0)
    ci = lax.broadcasted_iota(jnp.int32, (CH, CH), 1)
    eye = (ri == ci).astype(F32)
    strict = ri > ci
    ri2 = lax.broadcasted_iota(jnp.int32, (CH, LANE), 0)
    ci2 = lax.broadcasted_iota(jnp.int32, (CH, LANE), 1) % CH
    incl2 = ri2 >= ci2
    lo = lax.broadcasted_iota(jnp.int32, (1, LANE), 1) < n
    vi = lax.broadcasted_iota(jnp.int32, (LANE, LANE), 0) // n
    ki = lax.broadcasted_iota(jnp.int32, (LANE, LANE), 1) // n
    same_head = (vi == ki).astype(F32)
    same_blk = blk_ref[...]

    r, lw, k, v = r_ref[...], lw_ref[...], k_ref[...], v_ref[...]
    na, bb = na_ref[...], bb_ref[...]
    cums = _mask_dot(cm_ref[...], lw)
    cum, rest = cums[:CH], cums[CH:]
    e_cum = jnp.exp(cum)
    e_neg = jnp.exp(-cum)
    e_rest = jnp.exp(rest)
    rt = r * e_cum
    at = na * jnp.exp(cum - lw)
    bt = bb * e_neg
    kt = k * e_neg
    bh = bb * e_rest
    kh = k * e_rest

    pairs = [slice(p * LANE, (p + 1) * LANE) for p in range(npair)]
    pm = []
    for ps in pairs:
        a_p, r_p = at[:, ps], rt[:, ps]
        lhs = jnp.concatenate([jnp.where(lo, a_p, 0.0), jnp.where(lo, r_p, 0.0),
                               jnp.where(lo, 0.0, a_p), jnp.where(lo, 0.0, r_p)], axis=0)
        pm.append(_bdot_nt(lhs, jnp.concatenate([bt[:, ps], kt[:, ps]], axis=0)))
    heads = [(p, s) for p in range(npair) for s in range(2)]
    a_ab = [jnp.where(strict, pm[p][2 * s * CH:(2 * s + 1) * CH, :CH], 0.0) for p, s in heads]
    a_ak = [jnp.where(strict, pm[p][2 * s * CH:(2 * s + 1) * CH, CH:], 0.0) for p, s in heads]
    a_rbk = [jnp.where(incl2, pm[p][(2 * s + 1) * CH:(2 * s + 2) * CH, :], 0.0) for p, s in heads]
    t_inv = _inv_unit_lower([-a for a in a_ab], same_blk, eye)
    akv = [_bdot(a_ak[i], v[:, pairs[p]]) for i, (p, s) in enumerate(heads)]
    sol = [_bdot(t_inv[i], jnp.concatenate([at[:, pairs[p]], akv[i]], axis=1))
           for i, (p, s) in enumerate(heads)]
    sol = [jnp.where(jnp.concatenate([lo, lo], axis=1), sol[2 * p], sol[2 * p + 1]) for p in range(npair)]
    ht = [h_ref[p] for p in range(npair)]
    m = [_bdot_nt(jnp.concatenate([sol[p][:, :LANE], rt[:, pairs[p]]], axis=0), ht[p]) for p in range(npair)]
    uv = [jnp.concatenate([m[p][:CH] + sol[p][:, LANE:], v[:, pairs[p]]], axis=0) for p in range(npair)]
    yy = [_bdot(a_rbk[i], uv[p]) for i, (p, s) in enumerate(heads)]
    upd = [_bdot_tn(uv[p], jnp.concatenate([bh[:, pairs[p]], kh[:, pairs[p]]], axis=0)) for p in range(npair)]
    for p, ps in enumerate(pairs):
        h_ref[p] = ht[p] * e_cum[CH - 1:CH, ps] + upd[p] * same_head
        y = m[p][CH:] + jnp.where(lo, yy[2 * p], yy[2 * p + 1])

        def head_sum(t):
            s0 = jnp.sum(jnp.where(lo, t, 0.0), axis=-1, keepdims=True)
            s1 = jnp.sum(jnp.where(lo, 0.0, t), axis=-1, keepdims=True)
            return jnp.where(lo, s0, s1)

        y = y - head_sum(y) * (1.0 / n)
        y = y * lax.rsqrt(head_sum(y * y) * (1.0 / n) + RWKV_GN_EPS)
        y = y * lnw_ref[:, ps] + lnb_ref[:, ps]
        bonus = head_sum(r[:, ps] * k[:, ps] * rk_ref[:, ps])
        o_ref[:, ps] = ((y + bonus * v[:, ps]) * g_ref[:, ps]).astype(o_ref.dtype)


def _rwkv(proj, mu_p, w0, w2p, a0, a2p, g2p, kk, ka, rk, ln_w, ln_b, batch, lp):
    m = proj.shape[0]
    nr = lp // ROWS
    c = np.arange(MIX_W)[:, None] // RWKV_N == np.arange(LANE)[None, :]
    seg = jnp.asarray(c.astype(np.float32), BF16)
    segt = jnp.asarray(c.T.astype(np.float32), BF16)
    vec = lambda t: t.reshape(1, -1)
    const2 = lambda shape: pl.BlockSpec(shape, lambda b, r: (0, 0))
    row_w = pl.BlockSpec((ROWS, MIX_W), lambda b, r: (b * nr + r, 0))
    wide = jax.ShapeDtypeStruct((m, MIX_W), F32)
    src = lambda width, col: pl.BlockSpec((ROWS, width), lambda b, r: (b * nr + r, col // width))
    r_, lw, k_, v_, na, bb, g = pl.pallas_call(
        _rwkv_prep_kernel,
        grid=(batch, nr),
        in_specs=[src(MIX_W, COL_RWKV), src(MIX_W, COL_RWKV + MIX_W), src(MIX_W, COL_RWKV + 2 * MIX_W),
                  src(LANE, COL_RWKV_WA), src(2 * LANE, COL_RWKV_G),
                  const2((1, RWKV_PW)), const2((1, MIX_W)), const2((1, MIX_W)), const2((1, MIX_W)),
                  const2((1, MIX_W)), const2((LANE, MIX_W)), const2((LANE, MIX_W)),
                  const2((2 * LANE, MIX_W)), const2((MIX_W, LANE)), const2((LANE, MIX_W))],
        out_specs=[row_w] * 7,
        out_shape=[wide] * 7,
        scratch_shapes=[pltpu.VMEM((8, RWKV_PW), F32)],
        compiler_params=_cparams(("parallel", "arbitrary")),
        name="rwkv_prep",
    )(proj, proj, proj, proj, proj, vec(mu_p), vec(w0), vec(a0), vec(kk), vec(ka), w2p, a2p, g2p, seg, segt)

    i, j = _np_masks()
    cm = jnp.asarray(np.concatenate([(j <= i), (j > i)], axis=0).astype(np.float32), BF16)
    nc = lp // CH
    chunk = pl.BlockSpec((CH, MIX_W), lambda b, c: (b * nc + c, 0))
    return pl.pallas_call(
        _rwkv_scan_kernel,
        grid=(batch, nc),
        in_specs=[chunk] * 7 + [const2((1, MIX_W))] * 3 + [const2((2 * CH, CH)), const2((CH, CH))],
        out_specs=chunk,
        out_shape=jax.ShapeDtypeStruct((m, MIX_W), BF16),
        scratch_shapes=[pltpu.VMEM((MIX_W // LANE, LANE, LANE), F32)],
        compiler_params=_cparams(("parallel", "arbitrary")),
        name="rwkv_scan",
    )(r_, lw, k_, v_, na, bb, g, vec(rk), vec(ln_w), vec(ln_b), cm, _const_same_blk())


def _pad_cols(w, width):
    return jnp.pad(w, ((0, 0),) * (w.ndim - 1) + ((0, width - w.shape[-1]),))


def _pad_rows(w, rows, offset=0):
    return jnp.pad(w, ((offset, rows - offset - w.shape[0]), (0, 0)))


def _reorder_w_in(w_in):
    ret_w = 4 * MIX_W
    gla = ret_w
    gla_v = gla + 2 * GLA_HEADS * GLA_DK
    gla_lr = gla_v + 2 * GLA_HEADS * GLA_DV
    gdn = gla_lr + GLA_RANK
    gdn_ab = gdn + 4 * MIX_W
    rwkv = gdn_ab + 2 * GDN_HEADS
    rwkv_wa = rwkv + 3 * MIX_W
    rwkv_g = rwkv_wa + RWKV_W_RANK + RWKV_A_RANK
    w = w_in.astype(BF16)
    pieces = [w[..., :ret_w], w[..., gdn:gdn_ab], w[..., rwkv:rwkv_wa], w[..., gla_v:gla_lr],
              w[..., gla:gla_v], _pad_cols(w[..., gdn_ab:rwkv], LANE), w[..., rwkv_wa:rwkv_g],
              _pad_cols(w[..., rwkv_g:], 2 * LANE), _pad_cols(w[..., gla_lr:gdn], PROJ_W - COL_GLA_LR)]
    out = jnp.concatenate(pieces, axis=-1)
    assert out.shape[-1] == PROJ_W, out.shape
    return out


def _mixer_sublayer(x, h, lyr, big, layer, cos, sin, batch, lp, next_g):
    proj, w_gate = _matmul(h, big["w_proj"], layer, 1040, PROJ_TN, F32, "proj", (big["w_gate"], layer))

    o_a = _retention(proj, cos, sin, lyr["ret_norm"], batch, lp)
    o_b = _gla(proj, _pad_rows(lyr["gla_w2"], LANE).astype(BF16), lyr["gla_b"], lyr["gla_norm"],
               batch, lp)
    o_c = _gdn(proj, lyr["gdn_conv"], lyr["gdn_a_log"], lyr["gdn_dt_bias"], lyr["gdn_norm"], batch, lp)
    mu_p = jnp.pad(lyr["rwkv_mu"], (0, RWKV_PW - lyr["rwkv_mu"].shape[0]))
    w2p = _pad_rows(lyr["rwkv_w2"], LANE).astype(BF16)
    a2p = _pad_rows(lyr["rwkv_a2"], LANE, RWKV_W_RANK).astype(BF16)
    g2p = _pad_rows(lyr["rwkv_g2"], 2 * LANE).astype(BF16)
    o_d = _rwkv(proj, mu_p, lyr["rwkv_w0"], w2p, lyr["rwkv_a0"], a2p, g2p, lyr["rwkv_kk"],
                lyr["rwkv_ka"], lyr["rwkv_rk"], lyr["rwkv_ln_w"], lyr["rwkv_ln_b"], batch, lp)

    o4 = jnp.stack([o_a, o_b, o_c, o_d], axis=0)
    merged, w_up = _gate_merge(h, o4, w_gate, big["w_branch"], (big["w_up"], layer))
    z = _matmul(merged, big["w_out"], layer, 1040, 1024, F32, "w_out")
    x, h = _norm_residual(z, x, lyr["post_mix"], next_g)
    return x, h, w_up


def _ffn_sublayer(x, h, w_up, lyr, big, layer, lp, next_g):
    act, w_down = _ffn_up(h, w_up, big["ffn_conv"], lp, (big["w_down"], layer))
    z = _matmul(act, w_down, None, 520, 512, F32, "ffn_down")
    return _norm_residual(z, x, lyr["post_ffn"], next_g)


_LAYER_KEYS = ("pre_mix", "w_in", "ret_norm", "gla_w2", "gla_b", "gla_norm", "gdn_conv", "gdn_a_log",
               "gdn_dt_bias", "gdn_norm", "rwkv_mu", "rwkv_w0", "rwkv_w2", "rwkv_a0", "rwkv_a2", "rwkv_g2",
               "rwkv_kk", "rwkv_ka", "rwkv_rk", "rwkv_ln_w", "rwkv_ln_b", "w_branch", "w_gate", "w_out",
               "post_mix", "pre_ffn", "w_up", "ffn_conv", "w_down", "post_ffn")
_BIG_KEYS = ("w_in", "w_gate", "w_branch", "w_out", "w_up", "w_down", "ffn_conv")


def _big_weights(params):
    big = {k: params[k] for k in ("w_gate", "w_up", "w_down", "ffn_conv")}
    big["w_branch"] = params["w_branch"].astype(BF16)
    big["w_out"] = params["w_out"].astype(BF16)
    big["w_proj"] = _reorder_w_in(params["w_in"])
    return big


def _trunk(x, meta, params):
    batch, seq, d = x.shape
    depth = params["pre_mix"].shape[0]
    l = N_META + seq
    lp = -(-l // ROWS) * ROWS
    hcat = jnp.concatenate([jnp.broadcast_to(meta.astype(x.dtype)[None], (batch, N_META, d)), x,
                            jnp.zeros((batch, lp - l, d), x.dtype)], axis=1)
    xr = hcat.reshape(batch * lp, d)

    half = RET_D // 2
    inv_freq = ROPE_BASE ** (-jnp.arange(half, dtype=F32) / half)
    ang = jnp.arange(lp, dtype=F32)[:, None] * inv_freq[None, :]
    cos, sin = jnp.cos(ang), jnp.sin(ang)

    big = _big_weights(params)
    layers = [{k: params[k][i] for k in _LAYER_KEYS if k not in _BIG_KEYS} for i in range(depth)]
    h = _prenorm(xr, layers[0]["pre_mix"])
    for i, lyr in enumerate(layers):
        xr, h, w_up = _mixer_sublayer(xr, h, lyr, big, i, cos, sin, batch, lp, lyr["pre_ffn"])
        next_g = layers[i + 1]["pre_mix"] if i + 1 < depth else lyr["pre_ffn"]
        xr, h = _ffn_sublayer(xr, h, w_up, lyr, big, i, lp, next_g)
    return xr.reshape(batch, lp, d)[:, N_META:l]


def kernel(x, meta, pre_mix, w_in, ret_norm, gla_w2, gla_b, gla_norm, gdn_conv, gdn_a_log, gdn_dt_bias,
           gdn_norm, rwkv_mu, rwkv_w0, rwkv_w2, rwkv_a0, rwkv_a2, rwkv_g2, rwkv_kk, rwkv_ka, rwkv_rk,
           rwkv_ln_w, rwkv_ln_b, w_branch, w_gate, w_out, post_mix, pre_ffn, w_up, ffn_conv, w_down, post_ffn):
    params = dict(zip(_LAYER_KEYS, (pre_mix, w_in, ret_norm, gla_w2, gla_b, gla_norm, gdn_conv, gdn_a_log,
                                    gdn_dt_bias, gdn_norm, rwkv_mu, rwkv_w0, rwkv_w2, rwkv_a0, rwkv_a2,
                                    rwkv_g2, rwkv_kk, rwkv_ka, rwkv_rk, rwkv_ln_w, rwkv_ln_b, w_branch,
                                    w_gate, w_out, post_mix, pre_ffn, w_up, ffn_conv, w_down, post_ffn)))
    return _trunk(x, meta, params)
```

```python
import functools
import math

import numpy as np
import jax
import jax.numpy as jnp
from jax import lax
from jax.experimental import pallas as pl
from jax.experimental.pallas import tpu as pltpu

F32 = jnp.float32
BF16 = jnp.bfloat16

D_MODEL = 4096
N_META = 16
N_BRANCH = 4
MIX_W = D_MODEL // N_BRANCH
EPS = 1e-6
D_FF = 11008
FFN_CONV = 3

RET_HEADS, RET_D = 4, 256
ROPE_BASE = 10000.0
GLA_HEADS, GLA_DK, GLA_DV, GLA_RANK, GLA_TAU = 4, 128, 256, 16, 16.0
GDN_HEADS, GDN_D, GDN_CONV = 8, 128, 4
RWKV_HEADS, RWKV_N = 16, 64
RWKV_W_RANK, RWKV_A_RANK, RWKV_G_RANK = 64, 64, 160
RWKV_DECAY_SCALE = 0.606531
RWKV_GN_EPS = 64e-5

LANE = 128
CH = 64
SUB = 16
ROWS = 320
VMEM_LIMIT = 60 * 1024 * 1024

RWKV_LR = 3 * MIX_W
RWKV_PW = 3 * MIX_W + LANE + 2 * LANE
COL_RET = 0
COL_GDN = COL_RET + 4 * MIX_W
COL_RWKV = COL_GDN + 4 * MIX_W
COL_GLA_VG = COL_RWKV + 3 * MIX_W
COL_GLA_QK = COL_GLA_VG + 2 * MIX_W
COL_GDN_AB = COL_GLA_QK + 2 * GLA_HEADS * GLA_DK
COL_RWKV_WA = COL_GDN_AB + LANE
COL_RWKV_G = COL_RWKV_WA + LANE
COL_GLA_LR = COL_RWKV_G + 2 * LANE
PROJ_TN = 1024
PROJ_W = -(-(COL_GLA_LR + LANE) // PROJ_TN) * PROJ_TN


def _row_tile(m, pref):
    best = 16
    for t in range(16, min(m, pref) + 1, 16):
        if m % t == 0:
            best = t
    assert m % best == 0, (m, pref)
    return best


def _cparams(sem):
    return pltpu.CompilerParams(dimension_semantics=sem, vmem_limit_bytes=VMEM_LIMIT)


def _bdot(a, b):
    return jnp.dot(a.astype(BF16), b.astype(BF16), preferred_element_type=F32)


def _bdot_nt(a, b):
    return lax.dot_general(a.astype(BF16), b.astype(BF16), (((1,), (1,)), ((), ())),
                           preferred_element_type=F32)


def _bdot_tn(a, b):
    return lax.dot_general(a.astype(BF16), b.astype(BF16), (((0,), (0,)), ((), ())),
                           preferred_element_type=F32)


def _mask_dot(mask, x):
    hi = x.astype(BF16)
    lo = (x - hi.astype(F32)).astype(BF16)
    return (jnp.dot(mask, hi, preferred_element_type=F32)
            + jnp.dot(mask, lo, preferred_element_type=F32))


def _sigmoid(x):
    return 0.5 + 0.5 * jnp.tanh(0.5 * x)


def _silu(x):
    return x * _sigmoid(x)


def _softplus(x):
    return jnp.maximum(x, 0.0) + jnp.log1p(jnp.exp(-jnp.abs(x)))


def _inv_unit_lower(mats, same_blk, eye):
    each = lambda f, *ls: [f(*t) for t in zip(*ls)]
    d = [a * same_blk for a in mats]
    l = each(lambda a, dd: a - dd, mats, d)
    d2 = each(_bdot, d, d)
    d4 = each(_bdot, d2, d2)
    d8 = each(_bdot, d4, d4)
    p = [eye - dd for dd in d]
    p = each(lambda pp, m: pp + _bdot(pp, m), p, d2)
    p = each(lambda pp, m: pp + _bdot(pp, m), p, d4)
    td = each(lambda pp, m: pp + _bdot(pp, m), p, d8)
    x = each(_bdot, td, l)
    x2 = each(_bdot, x, x)
    q = [eye - xx for xx in x]
    q = each(lambda qq, m: qq + _bdot(qq, m), q, x2)
    return each(_bdot, q, td)


def _np_masks():
    i = np.arange(CH)[:, None]
    j = np.arange(CH)[None, :]
    return i, j


def _const_tri():
    i, j = _np_masks()
    return jnp.asarray((j <= i).astype(np.float32), BF16)


def _const_same_blk():
    i, j = _np_masks()
    return jnp.asarray((i // SUB == j // SUB).astype(np.float32), F32)


def _prenorm_kernel(x_ref, g_ref, o_ref):
    x = x_ref[...]
    y = x * lax.rsqrt(jnp.mean(x * x, axis=-1, keepdims=True) + EPS)
    o_ref[...] = (y * g_ref[...]).astype(o_ref.dtype)


def _prenorm(x, g):
    m, d = x.shape
    tm = _row_tile(m, 416)
    return pl.pallas_call(
        _prenorm_kernel,
        grid=(m // tm,),
        in_specs=[pl.BlockSpec((tm, d), lambda i: (i, 0)),
                  pl.BlockSpec((1, d), lambda i: (0, 0))],
        out_specs=pl.BlockSpec((tm, d), lambda i: (i, 0)),
        out_shape=jax.ShapeDtypeStruct((m, d), BF16),
        compiler_params=_cparams(("parallel",)),
        name="prenorm",
    )(x, g.reshape(1, d))


def _norm_residual_kernel(z_ref, x_ref, pg_ref, ng_ref, xo_ref, ho_ref):
    z = z_ref[...]
    y = z * lax.rsqrt(jnp.mean(z * z, axis=-1, keepdims=True) + EPS) * pg_ref[...]
    xn = x_ref[...] + y
    xo_ref[...] = xn
    h = xn * lax.rsqrt(jnp.mean(xn * xn, axis=-1, keepdims=True) + EPS) * ng_ref[...]
    ho_ref[...] = h.astype(ho_ref.dtype)


def _residual_kernel(z_ref, x_ref, pg_ref, xo_ref):
    z = z_ref[...]
    xo_ref[...] = x_ref[...] + z * lax.rsqrt(jnp.mean(z * z, axis=-1, keepdims=True) + EPS) * pg_ref[...]


def _norm_residual(z, x, post_g, next_g):
    m, d = x.shape
    tm = _row_tile(m, 208)
    row = pl.BlockSpec((tm, d), lambda i: (i, 0))
    vec = pl.BlockSpec((1, d), lambda i: (0, 0))
    if next_g is None:
        return pl.pallas_call(
            _residual_kernel, grid=(m // tm,), in_specs=[row, row, vec], out_specs=row,
            out_shape=jax.ShapeDtypeStruct((m, d), F32), compiler_params=_cparams(("parallel",)),
            name="residual",
        )(z, x, post_g.reshape(1, d)), None
    return pl.pallas_call(
        _norm_residual_kernel,
        grid=(m // tm,),
        in_specs=[row, row, vec, vec],
        out_specs=[row, row],
        out_shape=[jax.ShapeDtypeStruct((m, d), F32), jax.ShapeDtypeStruct((m, d), BF16)],
        compiler_params=_cparams(("parallel",)),
        name="norm_residual",
    )(z, x, post_g.reshape(1, d), next_g.reshape(1, d))


class _SideCast:
    def __init__(self, src, layer, steps):
        self.r, self.c = src.shape[1:]
        best = None
        for split in (1, 2, 4):
            if split > 1 and self.c % (split * LANE):
                continue
            for br in range(16, self.r + 1, 16):
                nblk = (self.r // br) * split
                if self.r % br == 0 and nblk <= steps and (best is None or nblk > best[0]):
                    best = (nblk, br, self.c // split)
        assert best is not None, (src.shape, steps)
        self.nblk, self.br, self.bc = best
        self.ncol = self.c // self.bc
        self.src, self.layer = src, layer

    def _block(self, step):
        b = jnp.minimum(step, self.nblk - 1)
        return b // self.ncol, b % self.ncol

    def in_spec(self, step_of):
        return pl.BlockSpec((None, self.br, self.bc), lambda *g: (self.layer,) + self._block(step_of(*g)))

    def out_spec(self, step_of):
        return pl.BlockSpec((self.br, self.bc), lambda *g: self._block(step_of(*g)))

    def out_shape(self):
        return jax.ShapeDtypeStruct((self.r, self.c), BF16)


def _side_cast(step, nblk, src_ref, dst_ref):
    @pl.when(step < nblk)
    def _():
        dst_ref[...] = src_ref[...].astype(dst_ref.dtype)


def _matmul_kernel(x_ref, w_ref, o_ref):
    o_ref[...] = jnp.dot(x_ref[...], w_ref[...], preferred_element_type=F32).astype(o_ref.dtype)


def _matmul_cast_kernel(x_ref, w_ref, src_ref, o_ref, dst_ref, *, nj, nblk):
    o_ref[...] = jnp.dot(x_ref[...], w_ref[...], preferred_element_type=F32).astype(o_ref.dtype)
    _side_cast(pl.program_id(0) * nj + pl.program_id(1), nblk, src_ref, dst_ref)


def _matmul(x, w, layer, tm, tn, out_dtype, name, cast=None):
    m, k = x.shape
    n = w.shape[-1]
    tm = _row_tile(m, tm)
    assert n % tn == 0, (n, tn)
    nj = n // tn
    w_spec = (pl.BlockSpec((k, tn), lambda i, j: (0, j)) if layer is None
              else pl.BlockSpec((None, k, tn), lambda i, j: (layer, 0, j)))
    x_spec = pl.BlockSpec((tm, k), lambda i, j: (i, 0))
    o_spec = pl.BlockSpec((tm, tn), lambda i, j: (i, j))
    o_shape = jax.ShapeDtypeStruct((m, n), out_dtype)
    if cast is None:
        return pl.pallas_call(
            _matmul_kernel, grid=(m // tm, nj), in_specs=[x_spec, w_spec], out_specs=o_spec,
            out_shape=o_shape, compiler_params=_cparams(("parallel", "parallel")), name=name,
        )(x, w)
    side = _SideCast(cast[0], cast[1], (m // tm) * nj)
    step_of = lambda i, j: i * nj + j
    x_spec = pl.BlockSpec((tm, k), lambda i, j: (i, 0), pipeline_mode=pl.Buffered(1))
    return pl.pallas_call(
        functools.partial(_matmul_cast_kernel, nj=nj, nblk=side.nblk),
        grid=(m // tm, nj),
        in_specs=[x_spec, w_spec, side.in_spec(step_of)],
        out_specs=[o_spec, side.out_spec(step_of)],
        out_shape=[o_shape, side.out_shape()],
        compiler_params=_cparams(("arbitrary", "arbitrary")),
        name=name,
    )(x, w, side.src)


def _gate_merge_kernel(h_ref, o_ref, wg_ref, wb_ref, src_ref, out_ref, dst_ref, acc_ref, *, nj, nblk):
    n = pl.program_id(2)
    gate = _sigmoid(jnp.dot(h_ref[...], wg_ref[...], preferred_element_type=F32))
    y = jnp.dot(o_ref[...], wb_ref[...], preferred_element_type=F32)

    @pl.when(n == 0)
    def _():
        acc_ref[...] = gate * y

    @pl.when(n > 0)
    def _():
        acc_ref[...] += gate * y

    @pl.when(n == N_BRANCH - 1)
    def _():
        out_ref[...] = acc_ref[...].astype(out_ref.dtype)

    _side_cast((pl.program_id(0) * nj + pl.program_id(1)) * N_BRANCH + n, nblk, src_ref, dst_ref)


def _gate_merge(h, o4, w_gate, w_branch, cast, tn=1024):
    m, d = h.shape
    tm = _row_tile(m, 1040)
    nj = d // tn
    layer = cast[1]
    side = _SideCast(cast[0], layer, (m // tm) * nj * N_BRANCH)
    step_of = lambda i, j, n: (i * nj + j) * N_BRANCH + n
    return pl.pallas_call(
        functools.partial(_gate_merge_kernel, nj=nj, nblk=side.nblk),
        grid=(m // tm, nj, N_BRANCH),
        in_specs=[pl.BlockSpec((tm, d), lambda i, j, n: (i, 0), pipeline_mode=pl.Buffered(1)),
                  pl.BlockSpec((None, tm, MIX_W), lambda i, j, n: (n, i, 0)),
                  pl.BlockSpec((d, tn), lambda i, j, n: (0, n * nj + j)),
                  pl.BlockSpec((None, None, MIX_W, tn), lambda i, j, n: (layer, n, 0, j)),
                  side.in_spec(step_of)],
        out_specs=[pl.BlockSpec((tm, tn), lambda i, j, n: (i, j)), side.out_spec(step_of)],
        out_shape=[jax.ShapeDtypeStruct((m, d), BF16), side.out_shape()],
        scratch_shapes=[pltpu.VMEM((tm, tn), F32)],
        compiler_params=_cparams(("arbitrary", "arbitrary", "arbitrary")),
        name="gate_merge",
    )(h, o4, w_gate, w_branch, side.src)


def _ffn_up_kernel(h_ref, halo_ref, wa_ref, wb_ref, ca_ref, cb_ref, src_ref, o_ref, dst_ref,
                   *, tiles_per_seq, nj, nblk):
    i = pl.program_id(0)
    first = (i % tiles_per_seq) == 0
    h = h_ref[...]
    halo = halo_ref[...]
    row8 = lax.broadcasted_iota(jnp.int32, (8, 1), 0)

    def conv(w_ref, c_ref):
        u = jnp.dot(h, w_ref[...], preferred_element_type=F32)
        uh = jnp.dot(halo, w_ref[...], preferred_element_type=F32)
        uh = jnp.where(first, 0.0, uh)
        p1, p2 = uh[15:16], uh[14:15]
        u1 = pltpu.roll(u, 1, 0)
        u2 = pltpu.roll(u, 2, 0)
        c = c_ref[...]
        y = c[2:3] * u + c[1:2] * u1 + c[0:1] * u2
        f1 = jnp.where(row8 == 0, p1, u1[:8])
        f2 = jnp.where(row8 == 0, p2, jnp.where(row8 == 1, p1, u2[:8]))
        top = c[2:3] * u[:8] + c[1:2] * f1 + c[0:1] * f2
        return jnp.concatenate([top, y[8:]], axis=0)

    a = _silu(conv(wa_ref, ca_ref))
    b = conv(wb_ref, cb_ref)
    o_ref[...] = (a * b).astype(o_ref.dtype)
    _side_cast(i * nj + pl.program_id(1), nblk, src_ref, dst_ref)


def _ffn_up(h, w_up, conv_w, lp, cast, tn=256):
    m, d = h.shape
    tm = _row_tile(lp, 2080)
    nj = D_FF // tn
    halo_blk = tm // 16
    side = _SideCast(cast[0], cast[1], (m // tm) * nj)
    step_of = lambda i, j: i * nj + j
    layer = cast[1]
    kern = functools.partial(_ffn_up_kernel, tiles_per_seq=lp // tm, nj=nj, nblk=side.nblk)
    return pl.pallas_call(
        kern,
        grid=(m // tm, nj),
        in_specs=[pl.BlockSpec((tm, d), lambda i, j: (i, 0), pipeline_mode=pl.Buffered(1)),
                  pl.BlockSpec((16, d), lambda i, j: (jnp.maximum(i * halo_blk - 1, 0), 0)),
                  pl.BlockSpec((d, tn), lambda i, j: (0, j)),
                  pl.BlockSpec((d, tn), lambda i, j: (0, nj + j)),
                  pl.BlockSpec((None, FFN_CONV, tn), lambda i, j: (layer, 0, j)),
                  pl.BlockSpec((None, FFN_CONV, tn), lambda i, j: (layer, 0, nj + j)),
                  side.in_spec(step_of)],
        out_specs=[pl.BlockSpec((tm, tn), lambda i, j: (i, j)), side.out_spec(step_of)],
        out_shape=[jax.ShapeDtypeStruct((m, D_FF), BF16), side.out_shape()],
        compiler_params=_cparams(("arbitrary", "arbitrary")),
        name="ffn_up",
    )(h, h, w_up, w_up, conv_w, conv_w, side.src)


def _retention_kernel(q_ref, k_ref, v_ref, g_ref, cos_ref, sin_ref, lg_ref, gain_ref, o_ref, s_ref):
    r = pl.program_id(2)

    @pl.when(r == 0)
    def _():
        s_ref[...] = jnp.zeros_like(s_ref)

    half = RET_D // 2
    cos, sin = cos_ref[...], sin_ref[...]

    def rope(t):
        t1, t2 = t[:, :half], t[:, half:]
        return jnp.concatenate([t1 * cos - t2 * sin, t1 * sin + t2 * cos], axis=1)

    q = rope(q_ref[...]) * RET_D ** -0.5
    k = rope(k_ref[...])
    v = v_ref[...]
    n = q.shape[0]
    lg = lg_ref[...][:, :1]
    ri = lax.broadcasted_iota(jnp.int32, (n, n), 0)
    ci = lax.broadcasted_iota(jnp.int32, (n, n), 1)
    rel = (ri - ci).astype(F32)
    decay = jnp.where(rel >= 0, jnp.exp(lg * jnp.maximum(rel, 0.0)), 0.0)
    idx = lax.broadcasted_iota(jnp.int32, (n, 1), 0).astype(F32)
    q_decay = jnp.exp(lg * (idx + 1.0))
    k_decay = jnp.exp(lg * (n - 1.0 - idx))
    s = s_ref[...]
    scores = _bdot_nt(q, k) * decay
    o = _bdot(scores, v) + _bdot(q, s) * q_decay
    s_ref[...] = s * jnp.exp(lg * float(n)) + _bdot_tn(k * k_decay, v)
    o = o - jnp.mean(o, axis=-1, keepdims=True)
    o = o * lax.rsqrt(jnp.mean(o * o, axis=-1, keepdims=True) + EPS) * gain_ref[...]
    o_ref[...] = (o * _silu(g_ref[...])).astype(o_ref.dtype)


def _retention(proj, cos, sin, norm_gain, batch, lp):
    m = proj.shape[0]
    nr = lp // ROWS
    gamma = 1.0 - 2.0 ** (-5.0 - np.arange(RET_HEADS, dtype=np.float64))
    lg = jnp.asarray(np.broadcast_to(np.log(gamma)[:, None, None], (RET_HEADS, 1, LANE)), F32)

    def part(p):
        return pl.BlockSpec((ROWS, RET_D), lambda b, h, r: (b * nr + r, p * RET_HEADS + h))

    tab = pl.BlockSpec((ROWS, RET_D // 2), lambda b, h, r: (r, 0))
    return pl.pallas_call(
        _retention_kernel,
        grid=(batch, RET_HEADS, nr),
        in_specs=[part(0), part(1), part(2), part(3), tab, tab,
                  pl.BlockSpec((None, 1, LANE), lambda b, h, r: (h, 0, 0)),
                  pl.BlockSpec((1, RET_D), lambda b, h, r: (0, h))],
        out_specs=pl.BlockSpec((ROWS, RET_D), lambda b, h, r: (b * nr + r, h)),
        out_shape=jax.ShapeDtypeStruct((m, MIX_W), BF16),
        scratch_shapes=[pltpu.VMEM((RET_D, RET_D), F32)],
        compiler_params=_cparams(("parallel", "parallel", "arbitrary")),
        name="retention",
    )(proj, proj, proj, proj, cos, sin, lg, norm_gain.reshape(1, MIX_W))


GLA_LEVELS = 6


def _gla_consts():
    i, j = _np_masks()
    mats = [(j <= i), (j > i)]
    pair = []
    for lvl in range(GLA_LEVELS):
        s = 32 >> lvl
        blk = (i // (2 * s)) * (2 * s)
        mid = blk + s - 1
        second = (i - blk) >= s
        mats.append(second & (j > mid) & (j <= i))
        mats.append((~second) & (j > i) & (j <= mid))
        jb = (j // (2 * s)) * (2 * s)
        pair.append(second & (jb == blk) & ((j - jb) < s))
    pair.append(i == j)
    em = np.concatenate(mats, axis=0).astype(np.float32)
    pm = np.stack(pair, axis=0).astype(np.float32)
    return jnp.asarray(em, BF16), jnp.asarray(pm, F32)


def _gla_kernel(q_ref, k_ref, v_ref, g_ref, lr_ref, w2_ref, bias_ref, gain_ref, em_ref, pm_ref,
                o_ref, s_ref):
    @pl.when(pl.program_id(1) == 0)
    def _():
        s_ref[...] = jnp.zeros_like(s_ref)

    heads = range(GLA_HEADS)
    ks = [slice(h * GLA_DK, (h + 1) * GLA_DK) for h in heads]
    vs = [slice(h * GLA_DV, (h + 1) * GLA_DV) for h in heads]
    x = _bdot(lr_ref[...], w2_ref[...]) + bias_ref[...]
    la = (jnp.minimum(x, 0.0) - jnp.log1p(jnp.exp(-jnp.abs(x)))) / GLA_TAU
    ex = jnp.exp(_mask_dot(em_ref[...], la))
    e_b = ex[0:CH]
    e_rest = ex[CH:2 * CH]
    q = q_ref[...] * GLA_DK ** -0.5
    k = k_ref[...]
    v = [v_ref[:, s] for s in vs]
    scores = [_bdot_nt(q[:, s], k[:, s]) * pm_ref[GLA_LEVELS] for s in ks]
    for lvl in range(GLA_LEVELS):
        qe = q * ex[(2 + 2 * lvl) * CH:(3 + 2 * lvl) * CH]
        ke = k * ex[(3 + 2 * lvl) * CH:(4 + 2 * lvl) * CH]
        pm = pm_ref[lvl]
        scores = [scores[h] + _bdot_nt(qe[:, ks[h]], ke[:, ks[h]]) * pm for h in heads]
    st = [s_ref[h] for h in heads]
    qb = q * e_b
    kr = k * e_rest
    o = [_bdot(scores[h], v[h]) + _bdot_nt(qb[:, ks[h]], st[h]) for h in heads]
    upd = [_bdot_tn(v[h], kr[:, ks[h]]) for h in heads]
    for h in heads:
        s_ref[h] = st[h] * e_b[CH - 1:CH, ks[h]] + upd[h]
        oh = o[h] * lax.rsqrt(jnp.mean(o[h] * o[h], axis=-1, keepdims=True) + EPS) * gain_ref[:, vs[h]]
        o_ref[:, vs[h]] = (oh * _silu(g_ref[:, vs[h]])).astype(o_ref.dtype)


def _gla(proj, w2p, bias, norm_gain, batch, lp):
    m = proj.shape[0]
    nc = lp // CH
    em, pm = _gla_consts()
    qk_w = GLA_HEADS * GLA_DK
    const = lambda shape: pl.BlockSpec(shape, lambda b, c: (0,) * len(shape))
    return pl.pallas_call(
        _gla_kernel,
        grid=(batch, nc),
        in_specs=[pl.BlockSpec((CH, qk_w), lambda b, c: (b * nc + c, COL_GLA_QK // qk_w)),
                  pl.BlockSpec((CH, qk_w), lambda b, c: (b * nc + c, COL_GLA_QK // qk_w + 1)),
                  pl.BlockSpec((CH, MIX_W), lambda b, c: (b * nc + c, COL_GLA_VG // MIX_W)),
                  pl.BlockSpec((CH, MIX_W), lambda b, c: (b * nc + c, COL_GLA_VG // MIX_W + 1)),
                  pl.BlockSpec((CH, LANE), lambda b, c: (b * nc + c, COL_GLA_LR // LANE)),
                  const((LANE, qk_w)), const((1, qk_w)), const((1, MIX_W)),
                  const(em.shape), const(pm.shape)],
        out_specs=pl.BlockSpec((CH, MIX_W), lambda b, c: (b * nc + c, 0)),
        out_shape=jax.ShapeDtypeStruct((m, MIX_W), BF16),
        scratch_shapes=[pltpu.VMEM((GLA_HEADS, GLA_DV, GLA_DK), F32)],
        compiler_params=_cparams(("parallel", "arbitrary")),
        name="gla",
    )(proj, proj, proj, proj, proj, w2p, bias.reshape(1, -1), norm_gain.reshape(1, MIX_W), em, pm)


def _gdn_kernel(q_ref, k_ref, v_ref, z_ref, ab_ref, cq_ref, ck_ref, cv_ref, alog_ref, dtb_ref,
                gain_ref, tri_ref, blk_ref, o_ref, s_ref, carry_ref):
    @pl.when(pl.program_id(1) == 0)
    def _():
        s_ref[...] = jnp.zeros_like(s_ref)
        carry_ref[...] = jnp.zeros_like(carry_ref)

    def conv_silu(x_ref, c_ref, slot):
        x = x_ref[...]
        ext = jnp.concatenate([carry_ref[slot], x], axis=0)
        carry_ref[slot] = x[CH - 8:, :]
        c = c_ref[...]
        y = c[3:4] * x
        for t in range(1, GDN_CONV):
            y = y + c[3 - t:4 - t] * ext[8 - t:8 - t + CH, :]
        return _silu(y)

    def l2norm(t):
        return t * lax.rsqrt(jnp.sum(t * t, axis=-1, keepdims=True) + EPS)

    qc = conv_silu(q_ref, cq_ref, 0)
    kc = conv_silu(k_ref, ck_ref, 1)
    vc = conv_silu(v_ref, cv_ref, 2)
    ab = ab_ref[...]
    g_t = -jnp.exp(alog_ref[...]) * _softplus(ab + dtb_ref[...])
    beta_t = _sigmoid(ab)
    rep = lambda t, idx: jnp.broadcast_to(t[:, idx:idx + 1], (CH, LANE))
    heads = range(GDN_HEADS)
    hs = [slice(h * GDN_D, (h + 1) * GDN_D) for h in heads]
    gcum_all = _mask_dot(tri_ref[...], jnp.concatenate([rep(g_t, h) for h in heads], axis=1))

    same_blk = blk_ref[...]
    ri = lax.broadcasted_iota(jnp.int32, (CH, CH), 0)
    ci = lax.broadcasted_iota(jnp.int32, (CH, CH), 1)
    eye = (ri == ci).astype(F32)
    q = [l2norm(qc[:, s]) * GDN_D ** -0.5 for s in hs]
    k = [l2norm(kc[:, s]) for s in hs]
    v = [vc[:, s] for s in hs]
    beta = [rep(beta_t, GDN_HEADS + h) for h in heads]
    gcum = [gcum_all[:, s] for s in hs]
    grow = [jnp.transpose(g)[:1, :CH] for g in gcum]
    decay = [jnp.where(ri >= ci, jnp.exp(jnp.minimum(gcum[h][:, :1] - grow[h], 0.0)), 0.0) for h in heads]
    kb = [k[h] * beta[h] for h in heads]
    kq = [_bdot_nt(jnp.concatenate([kb[h], q[h]], axis=0), k[h]) for h in heads]
    a_mat = [jnp.where(ri > ci, kq[h][:CH] * decay[h], 0.0) for h in heads]
    attn = [kq[h][CH:] * decay[h] for h in heads]
    t_inv = _inv_unit_lower(a_mat, same_blk, eye)
    e_g = [jnp.exp(g) for g in gcum]
    sol = [_bdot(t_inv[h], jnp.concatenate([v[h] * beta[h], kb[h] * e_g[h]], axis=1)) for h in heads]
    s = [s_ref[h] for h in heads]
    m = [_bdot(jnp.concatenate([sol[h][:, GDN_D:], q[h] * e_g[h]], axis=0), s[h]) for h in heads]
    v_new = [sol[h][:, :GDN_D] - m[h][:CH] for h in heads]
    o = [m[h][CH:] + _bdot(attn[h], v_new[h]) for h in heads]
    g_last = [g[CH - 1:CH, :] for g in gcum]
    upd = [_bdot_tn(k[h] * jnp.exp(g_last[h] - gcum[h]), v_new[h]) for h in heads]
    gain = gain_ref[...]
    for h in heads:
        s_ref[h] = s[h] * jnp.exp(g_last[h]) + upd[h]
        oh = o[h] * lax.rsqrt(jnp.mean(o[h] * o[h], axis=-1, keepdims=True) + EPS) * gain
        o_ref[:, hs[h]] = (oh * _silu(z_ref[:, hs[h]])).astype(o_ref.dtype)


def _gdn(proj, conv_w, a_log, dt_bias, norm_gain, batch, lp):
    m = proj.shape[0]
    nc = lp // CH
    nh = GDN_HEADS
    pad = lambda t: jnp.pad(t.reshape(1, nh), ((0, 0), (0, LANE - nh)))
    part = lambda p: pl.BlockSpec((CH, MIX_W), lambda b, c: (b * nc + c, COL_GDN // MIX_W + p))
    cpart = lambda p: pl.BlockSpec((GDN_CONV, MIX_W), lambda b, c: (0, p))
    const = lambda shape: pl.BlockSpec(shape, lambda b, c: (0,) * len(shape))
    return pl.pallas_call(
        _gdn_kernel,
        grid=(batch, nc),
        in_specs=[part(0), part(1), part(2), part(3),
                  pl.BlockSpec((CH, LANE), lambda b, c: (b * nc + c, COL_GDN_AB // LANE)),
                  cpart(0), cpart(1), cpart(2),
                  const((1, LANE)), const((1, LANE)), const((1, GDN_D)),
                  const((CH, CH)), const((CH, CH))],
        out_specs=pl.BlockSpec((CH, MIX_W), lambda b, c: (b * nc + c, 0)),
        out_shape=jax.ShapeDtypeStruct((m, MIX_W), BF16),
        scratch_shapes=[pltpu.VMEM((nh, GDN_D, GDN_D), F32), pltpu.VMEM((3, 8, MIX_W), F32)],
        compiler_params=_cparams(("parallel", "arbitrary")),
        name="gdn",
    )(proj, proj, proj, proj, proj, conv_w, conv_w, conv_w, pad(a_log), pad(dt_bias),
      norm_gain.reshape(1, GDN_D), _const_tri(), _const_same_blk())


def _rwkv_prep_kernel(sr_ref, sk_ref, sv_ref, swa_ref, sg_ref, mu_ref, w0_ref, a0_ref, kk_ref, ka_ref,
                      w2_ref, a2_ref, g2_ref, seg_ref, segt_ref,
                      r_ref, lw_ref, k_ref, v_ref, na_ref, bb_ref, g_ref, carry_ref):
    rb = pl.program_id(1)

    @pl.when(rb == 0)
    def _():
        carry_ref[...] = jnp.zeros_like(carry_ref)

    x = jnp.concatenate([sr_ref[...], sk_ref[...], sv_ref[...], swa_ref[...], sg_ref[...]], axis=1)
    ext = jnp.concatenate([carry_ref[...], x], axis=0)
    carry_ref[...] = x[ROWS - 8:, :]
    prev = ext[7:7 + ROWS, :]
    s = x + (prev - x) * mu_ref[...]
    r = s[:, :MIX_W]
    k = s[:, MIX_W:2 * MIX_W]
    v = s[:, 2 * MIX_W:3 * MIX_W]
    wa_src = s[:, RWKV_LR:RWKV_LR + LANE]
    g_src = s[:, RWKV_LR + LANE:]
    lw_ref[...] = -RWKV_DECAY_SCALE * _sigmoid(w0_ref[...] + _bdot(jnp.tanh(wa_src), w2_ref[...]))
    a = _sigmoid(a0_ref[...] + _bdot(wa_src, a2_ref[...]))
    g_ref[...] = _bdot(_sigmoid(g_src), g2_ref[...])
    kk = k * kk_ref[...]
    sq = kk * kk
    sq_hi = sq.astype(BF16)
    sq_lo = (sq - sq_hi.astype(F32)).astype(BF16)
    seg = seg_ref[...]
    hsum = (jnp.dot(sq_hi, seg, preferred_element_type=F32)
            + jnp.dot(sq_lo, seg, preferred_element_type=F32))
    inv = lax.rsqrt(hsum + EPS)
    inv_hi = inv.astype(BF16)
    inv_lo = (inv - inv_hi.astype(F32)).astype(BF16)
    segt = segt_ref[...]
    inv_full = (jnp.dot(inv_hi, segt, preferred_element_type=F32)
                + jnp.dot(inv_lo, segt, preferred_element_type=F32))
    kk = kk * inv_full
    r_ref[...] = r
    k_ref[...] = k * (1.0 + (a - 1.0) * ka_ref[...])
    v_ref[...] = v
    na_ref[...] = -kk
    bb_ref[...] = kk * a


def _rwkv_scan_kernel(r_ref, lw_ref, k_ref, v_ref, na_ref, bb_ref, g_ref, rk_ref, lnw_ref, lnb_ref,
                      cm_ref, blk_ref, o_ref, h_ref):
    @pl.when(pl.program_id(1) == 0)
    def _():
        h_ref[...] = jnp.zeros_like(h_ref)

    n = RWKV_N
    npair = MIX_W // LANE
    ri = lax.broadcasted_iota(jnp.int32, (CH, CH), 0)
    ci = lax.broadcasted_iota(jnp.int32, (CH, CH), 1)
    eye = (ri == ci).astype(F32)
    strict = ri > ci
    ri2 = lax.broadcasted_iota(jnp.int32, (CH, LANE), 0)
    ci2 = lax.broadcasted_iota(jnp.int32, (CH, LANE), 1) % CH
    incl2 = ri2 >= ci2
    lo = lax.broadcasted_iota(jnp.int32, (1, LANE), 1) < n
    vi = lax.broadcasted_iota(jnp.int32, (LANE, LANE), 0) // n
    ki = lax.broadcasted_iota(jnp.int32, (LANE, LANE), 1) // n
    same_head = (vi == ki).astype(F32)
    same_blk = blk_ref[...]

    r, lw, k, v = r_ref[...], lw_ref[...], k_ref[...], v_ref[...]
    na, bb = na_ref[...], bb_ref[...]
    cums = _mask_dot(cm_ref[...], lw)
    cum, rest = cums[:CH], cums[CH:]
    e_cum = jnp.exp(cum)
    e_neg = jnp.exp(-cum)
    e_rest = jnp.exp(rest)
    rt = r * e_cum
    at = na * jnp.exp(cum - lw)
    bt = bb * e_neg
    kt = k * e_neg
    bh = bb * e_rest
    kh = k * e_rest

    pairs = [slice(p * LANE, (p + 1) * LANE) for p in range(npair)]
    pm = []
    for ps in pairs:
        a_p, r_p = at[:, ps], rt[:, ps]
        lhs = jnp.concatenate([jnp.where(lo, a_p, 0.0), jnp.where(lo, r_p, 0.0),
                               jnp.where(lo, 0.0, a_p), jnp.where(lo, 0.0, r_p)], axis=0)
        pm.append(_bdot_nt(lhs, jnp.concatenate([bt[:, ps], kt[:, ps]], axis=0)))
    heads = [(p, s) for p in range(npair) for s in range(2)]
    a_ab = [jnp.where(strict, pm[p][2 * s * CH:(2 * s + 1) * CH, :CH], 0.0) for p, s in heads]
    a_ak = [jnp.where(strict, pm[p][2 * s * CH:(2 * s + 1) * CH, CH:], 0.0) for p, s in heads]
    a_rbk = [jnp.where(incl2, pm[p][(2 * s + 1) * CH:(2 * s + 2) * CH, :], 0.0) for p, s in heads]
    t_inv = _inv_unit_lower([-a for a in a_ab], same_blk, eye)
    akv = [_bdot(a_ak[i], v[:, pairs[p]]) for i, (p, s) in enumerate(heads)]
    sol = [_bdot(t_inv[i], jnp.concatenate([at[:, pairs[p]], akv[i]], axis=1))
           for i, (p, s) in enumerate(heads)]
    sol = [jnp.where(jnp.concatenate([lo, lo], axis=1), sol[2 * p], sol[2 * p + 1]) for p in range(npair)]
    ht = [h_ref[p] for p in range(npair)]
    m = [_bdot_nt(jnp.concatenate([sol[p][:, :LANE], rt[:, pairs[p]]], axis=0), ht[p]) for p in range(npair)]
    uv = [jnp.concatenate([m[p][:CH] + sol[p][:, LANE:], v[:, pairs[p]]], axis=0) for p in range(npair)]
    yy = [_bdot(a_rbk[i], uv[p]) for i, (p, s) in enumerate(heads)]
    upd = [_bdot_tn(uv[p], jnp.concatenate([bh[:, pairs[p]], kh[:, pairs[p]]], axis=0)) for p in range(npair)]
    for p, ps in enumerate(pairs):
        h_ref[p] = ht[p] * e_cum[CH - 1:CH, ps] + upd[p] * same_head
        y = m[p][CH:] + jnp.where(lo, yy[2 * p], yy[2 * p + 1])

        def head_sum(t):
            s0 = jnp.sum(jnp.where(lo, t, 0.0), axis=-1, keepdims=True)
            s1 = jnp.sum(jnp.where(lo, 0.0, t), axis=-1, keepdims=True)
            return jnp.where(lo, s0, s1)

        y = y - head_sum(y) * (1.0 / n)
        y = y * lax.rsqrt(head_sum(y * y) * (1.0 / n) + RWKV_GN_EPS)
        y = y * lnw_ref[:, ps] + lnb_ref[:, ps]
        bonus = head_sum(r[:, ps] * k[:, ps] * rk_ref[:, ps])
        o_ref[:, ps] = ((y + bonus * v[:, ps]) * g_ref[:, ps]).astype(o_ref.dtype)


def _rwkv(proj, mu_p, w0, w2p, a0, a2p, g2p, kk, ka, rk, ln_w, ln_b, batch, lp):
    m = proj.shape[0]
    nr = lp // ROWS
    c = np.arange(MIX_W)[:, None] // RWKV_N == np.arange(LANE)[None, :]
    seg = jnp.asarray(c.astype(np.float32), BF16)
    segt = jnp.asarray(c.T.astype(np.float32), BF16)
    vec = lambda t: t.reshape(1, -1)
    const2 = lambda shape: pl.BlockSpec(shape, lambda b, r: (0, 0))
    row_w = pl.BlockSpec((ROWS, MIX_W), lambda b, r: (b * nr + r, 0))
    wide = jax.ShapeDtypeStruct((m, MIX_W), F32)
    src = lambda width, col: pl.BlockSpec((ROWS, width), lambda b, r: (b * nr + r, col // width))
    r_, lw, k_, v_, na, bb, g = pl.pallas_call(
        _rwkv_prep_kernel,
        grid=(batch, nr),
        in_specs=[src(MIX_W, COL_RWKV), src(MIX_W, COL_RWKV + MIX_W), src(MIX_W, COL_RWKV + 2 * MIX_W),
                  src(LANE, COL_RWKV_WA), src(2 * LANE, COL_RWKV_G),
                  const2((1, RWKV_PW)), const2((1, MIX_W)), const2((1, MIX_W)), const2((1, MIX_W)),
                  const2((1, MIX_W)), const2((LANE, MIX_W)), const2((LANE, MIX_W)),
                  const2((2 * LANE, MIX_W)), const2((MIX_W, LANE)), const2((LANE, MIX_W))],
        out_specs=[row_w] * 7,
        out_shape=[wide] * 7,
        scratch_shapes=[pltpu.VMEM((8, RWKV_PW), F32)],
        compiler_params=_cparams(("parallel", "arbitrary")),
        name="rwkv_prep",
    )(proj, proj, proj, proj, proj, vec(mu_p), vec(w0), vec(a0), vec(kk), vec(ka), w2p, a2p, g2p, seg, segt)

    i, j = _np_masks()
    cm = jnp.asarray(np.concatenate([(j <= i), (j > i)], axis=0).astype(np.float32), BF16)
    nc = lp // CH
    chunk = pl.BlockSpec((CH, MIX_W), lambda b, c: (b * nc + c, 0))
    return pl.pallas_call(
        _rwkv_scan_kernel,
        grid=(batch, nc),
        in_specs=[chunk] * 7 + [const2((1, MIX_W))] * 3 + [const2((2 * CH, CH)), const2((CH, CH))],
        out_specs=chunk,
        out_shape=jax.ShapeDtypeStruct((m, MIX_W), BF16),
        scratch_shapes=[pltpu.VMEM((MIX_W // LANE, LANE, LANE), F32)],
        compiler_params=_cparams(("parallel", "arbitrary")),
        name="rwkv_scan",
    )(r_, lw, k_, v_, na, bb, g, vec(rk), vec(ln_w), vec(ln_b), cm, _const_same_blk())


def _pad_cols(w, width):
    return jnp.pad(w, ((0, 0),) * (w.ndim - 1) + ((0, width - w.shape[-1]),))


def _pad_rows(w, rows, offset=0):
    return jnp.pad(w, ((offset, rows - offset - w.shape[0]), (0, 0)))


def _reorder_w_in(w_in):
    ret_w = 4 * MIX_W
    gla = ret_w
    gla_v = gla + 2 * GLA_HEADS * GLA_DK
    gla_lr = gla_v + 2 * GLA_HEADS * GLA_DV
    gdn = gla_lr + GLA_RANK
    gdn_ab = gdn + 4 * MIX_W
    rwkv = gdn_ab + 2 * GDN_HEADS
    rwkv_wa = rwkv + 3 * MIX_W
    rwkv_g = rwkv_wa + RWKV_W_RANK + RWKV_A_RANK
    w = w_in
    pieces = [w[..., :ret_w], w[..., gdn:gdn_ab], w[..., rwkv:rwkv_wa], w[..., gla_v:gla_lr],
              w[..., gla:gla_v], _pad_cols(w[..., gdn_ab:rwkv], LANE), w[..., rwkv_wa:rwkv_g],
              _pad_cols(w[..., rwkv_g:], 2 * LANE), _pad_cols(w[..., gla_lr:gdn], PROJ_W - COL_GLA_LR)]
    out = jnp.concatenate(pieces, axis=-1).astype(BF16)
    assert out.shape[-1] == PROJ_W, out.shape
    return out


def _mixer_sublayer(x, h, lyr, big, layer, cos, sin, batch, lp, next_g):
    proj, w_gate = _matmul(h, big["w_proj"][layer], None, 1040, PROJ_TN, F32, "proj", (big["w_gate"], layer))

    o_a = _retention(proj, cos, sin, lyr["ret_norm"], batch, lp)
    o_b = _gla(proj, _pad_rows(lyr["gla_w2"], LANE).astype(BF16), lyr["gla_b"], lyr["gla_norm"],
               batch, lp)
    o_c = _gdn(proj, lyr["gdn_conv"], lyr["gdn_a_log"], lyr["gdn_dt_bias"], lyr["gdn_norm"], batch, lp)
    mu_p = jnp.pad(lyr["rwkv_mu"], (0, RWKV_PW - lyr["rwkv_mu"].shape[0]))
    w2p = _pad_rows(lyr["rwkv_w2"], LANE).astype(BF16)
    a2p = _pad_rows(lyr["rwkv_a2"], LANE, RWKV_W_RANK).astype(BF16)
    g2p = _pad_rows(lyr["rwkv_g2"], 2 * LANE).astype(BF16)
    o_d = _rwkv(proj, mu_p, lyr["rwkv_w0"], w2p, lyr["rwkv_a0"], a2p, g2p, lyr["rwkv_kk"],
                lyr["rwkv_ka"], lyr["rwkv_rk"], lyr["rwkv_ln_w"], lyr["rwkv_ln_b"], batch, lp)

    o4 = jnp.stack([o_a, o_b, o_c, o_d], axis=0)
    merged, w_up = _gate_merge(h, o4, w_gate, big["w_branch"], (big["w_up"], layer))
    z = _matmul(merged, big["w_out"], layer, 1040, 1024, F32, "w_out")
    x, h = _norm_residual(z, x, lyr["post_mix"], next_g)
    return x, h, w_up


def _ffn_sublayer(x, h, w_up, lyr, big, layer, lp, next_g):
    act, w_down = _ffn_up(h, w_up, big["ffn_conv"], lp, (big["w_down"], layer))
    z = _matmul(act, w_down, None, 520, 512, F32, "ffn_down")
    return _norm_residual(z, x, lyr["post_ffn"], next_g)


_LAYER_KEYS = ("pre_mix", "w_in", "ret_norm", "gla_w2", "gla_b", "gla_norm", "gdn_conv", "gdn_a_log",
               "gdn_dt_bias", "gdn_norm", "rwkv_mu", "rwkv_w0", "rwkv_w2", "rwkv_a0", "rwkv_a2", "rwkv_g2",
               "rwkv_kk", "rwkv_ka", "rwkv_rk", "rwkv_ln_w", "rwkv_ln_b", "w_branch", "w_gate", "w_out",
               "post_mix", "pre_ffn", "w_up", "ffn_conv", "w_down", "post_ffn")
_BIG_KEYS = ("w_in", "w_gate", "w_branch", "w_out", "w_up", "w_down", "ffn_conv")


def _big_weights(params):
    big = {k: params[k] for k in ("w_gate", "w_up", "w_down", "ffn_conv")}
    big["w_branch"] = params["w_branch"].astype(BF16)
    big["w_out"] = params["w_out"].astype(BF16)
    big["w_proj"] = [_reorder_w_in(params["w_in"][i]) for i in range(params["w_in"].shape[0])]
    return big


def _trunk(x, meta, params):
    batch, seq, d = x.shape
    depth = params["pre_mix"].shape[0]
    l = N_META + seq
    lp = -(-l // ROWS) * ROWS
    hcat = jnp.concatenate([jnp.broadcast_to(meta.astype(x.dtype)[None], (batch, N_META, d)), x,
                            jnp.zeros((batch, lp - l, d), x.dtype)], axis=1)
    xr = hcat.reshape(batch * lp, d)

    half = RET_D // 2
    inv_freq = ROPE_BASE ** (-jnp.arange(half, dtype=F32) / half)
    ang = jnp.arange(lp, dtype=F32)[:, None] * inv_freq[None, :]
    cos, sin = jnp.cos(ang), jnp.sin(ang)

    big = _big_weights(params)
    layers = [{k: params[k][i] for k in _LAYER_KEYS if k not in _BIG_KEYS} for i in range(depth)]
    h = _prenorm(xr, layers[0]["pre_mix"])
    for i, lyr in enumerate(layers):
        xr, h, w_up = _mixer_sublayer(xr, h, lyr, big, i, cos, sin, batch, lp, lyr["pre_ffn"])
        next_g = layers[i + 1]["pre_mix"] if i + 1 < depth else None
        xr, h = _ffn_sublayer(xr, h, w_up, lyr, big, i, lp, next_g)
    return xr.reshape(batch, lp, d)[:, N_META:l]


def kernel(x, meta, pre_mix, w_in, ret_norm, gla_w2, gla_b, gla_norm, gdn_conv, gdn_a_log, gdn_dt_bias,
           gdn_norm, rwkv_mu, rwkv_w0, rwkv_w2, rwkv_a0, rwkv_a2, rwkv_g2, rwkv_kk, rwkv_ka, rwkv_rk,
           rwkv_ln_w, rwkv_ln_b, w_branch, w_gate, w_out, post_mix, pre_ffn, w_up, ffn_conv, w_down, post_ffn):
    params = dict(zip(_LAYER_KEYS, (pre_mix, w_in, ret_norm, gla_w2, gla_b, gla_norm, gdn_conv, gdn_a_log,
                                    gdn_dt_bias, gdn_norm, rwkv_mu, rwkv_w0, rwkv_w2, rwkv_a0, rwkv_a2,
                                    rwkv_g2, rwkv_kk, rwkv_ka, rwkv_rk, rwkv_ln_w, rwkv_ln_b, w_branch,
                                    w_gate, w_out, post_mix, pre_ffn, w_up, ffn_conv, w_down, post_ffn)))
    return _trunk(x, meta, params)
```

```python
import functools
import math

import numpy as np
import jax
import jax.numpy as jnp
from jax import lax
from jax.experimental import pallas as pl
from jax.experimental.pallas import tpu as pltpu

F32 = jnp.float32
BF16 = jnp.bfloat16

D_MODEL = 4096
N_META = 16
N_BRANCH = 4
MIX_W = D_MODEL // N_BRANCH
EPS = 1e-6
D_FF = 11008
FFN_CONV = 3

RET_HEADS, RET_D = 4, 256
ROPE_BASE = 10000.0
GLA_HEADS, GLA_DK, GLA_DV, GLA_RANK, GLA_TAU = 4, 128, 256, 16, 16.0
GDN_HEADS, GDN_D, GDN_CONV = 8, 128, 4
RWKV_HEADS, RWKV_N = 16, 64
RWKV_W_RANK, RWKV_A_RANK, RWKV_G_RANK = 64, 64, 160
RWKV_DECAY_SCALE = 0.606531
RWKV_GN_EPS = 64e-5

LANE = 128
CH = 64
SUB = 16
ROWS = 320
VMEM_LIMIT = 60 * 1024 * 1024

RWKV_LR = 3 * MIX_W
RWKV_PW = 3 * MIX_W + LANE + 2 * LANE
COL_RET = 0
COL_GDN = COL_RET + 4 * MIX_W
COL_RWKV = COL_GDN + 4 * MIX_W
COL_GLA_VG = COL_RWKV + 3 * MIX_W
COL_GLA_QK = COL_GLA_VG + 2 * MIX_W
COL_GDN_AB = COL_GLA_QK + 2 * GLA_HEADS * GLA_DK
COL_RWKV_WA = COL_GDN_AB + LANE
COL_RWKV_G = COL_RWKV_WA + LANE
COL_GLA_LR = COL_RWKV_G + 2 * LANE
PROJ_TN = 1024
PROJ_W = -(-(COL_GLA_LR + LANE) // PROJ_TN) * PROJ_TN


def _row_tile(m, pref):
    best = 16
    for t in range(16, min(m, pref) + 1, 16):
        if m % t == 0:
            best = t
    assert m % best == 0, (m, pref)
    return best


def _cparams(sem):
    return pltpu.CompilerParams(dimension_semantics=sem, vmem_limit_bytes=VMEM_LIMIT)


def _bdot(a, b):
    return jnp.dot(a.astype(BF16), b.astype(BF16), preferred_element_type=F32)


def _bdot_nt(a, b):
    return lax.dot_general(a.astype(BF16), b.astype(BF16), (((1,), (1,)), ((), ())),
                           preferred_element_type=F32)


def _bdot_tn(a, b):
    return lax.dot_general(a.astype(BF16), b.astype(BF16), (((0,), (0,)), ((), ())),
                           preferred_element_type=F32)


def _mask_dot(mask, x):
    hi = x.astype(BF16)
    lo = (x - hi.astype(F32)).astype(BF16)
    return (jnp.dot(mask, hi, preferred_element_type=F32)
            + jnp.dot(mask, lo, preferred_element_type=F32))


def _sigmoid(x):
    return 0.5 + 0.5 * jnp.tanh(0.5 * x)


def _silu(x):
    return x * _sigmoid(x)


def _softplus(x):
    return jnp.maximum(x, 0.0) + jnp.log1p(jnp.exp(-jnp.abs(x)))


def _inv_unit_lower(mats, same_blk, eye):
    each = lambda f, *ls: [f(*t) for t in zip(*ls)]
    d = [a * same_blk for a in mats]
    l = each(lambda a, dd: a - dd, mats, d)
    d2 = each(_bdot, d, d)
    d4 = each(_bdot, d2, d2)
    d8 = each(_bdot, d4, d4)
    p = [eye - dd for dd in d]
    p = each(lambda pp, m: pp + _bdot(pp, m), p, d2)
    p = each(lambda pp, m: pp + _bdot(pp, m), p, d4)
    td = each(lambda pp, m: pp + _bdot(pp, m), p, d8)
    x = each(_bdot, td, l)
    x2 = each(_bdot, x, x)
    q = [eye - xx for xx in x]
    q = each(lambda qq, m: qq + _bdot(qq, m), q, x2)
    return each(_bdot, q, td)


def _np_masks():
    i = np.arange(CH)[:, None]
    j = np.arange(CH)[None, :]
    return i, j


def _const_tri():
    i, j = _np_masks()
    return jnp.asarray((j <= i).astype(np.float32), BF16)


def _const_same_blk():
    i, j = _np_masks()
    return jnp.asarray((i // SUB == j // SUB).astype(np.float32), F32)


def _prenorm_kernel(x_ref, g_ref, o_ref):
    x = x_ref[...]
    y = x * lax.rsqrt(jnp.mean(x * x, axis=-1, keepdims=True) + EPS)
    o_ref[...] = (y * g_ref[...]).astype(o_ref.dtype)


def _prenorm(x, g):
    m, d = x.shape
    tm = _row_tile(m, 416)
    return pl.pallas_call(
        _prenorm_kernel,
        grid=(m // tm,),
        in_specs=[pl.BlockSpec((tm, d), lambda i: (i, 0)),
                  pl.BlockSpec((1, d), lambda i: (0, 0))],
        out_specs=pl.BlockSpec((tm, d), lambda i: (i, 0)),
        out_shape=jax.ShapeDtypeStruct((m, d), BF16),
        compiler_params=_cparams(("parallel",)),
        name="prenorm",
    )(x, g.reshape(1, d))


def _norm_residual_kernel(z_ref, x_ref, pg_ref, ng_ref, xo_ref, ho_ref):
    z = z_ref[...]
    y = z * lax.rsqrt(jnp.mean(z * z, axis=-1, keepdims=True) + EPS) * pg_ref[...]
    xn = x_ref[...] + y
    xo_ref[...] = xn
    h = xn * lax.rsqrt(jnp.mean(xn * xn, axis=-1, keepdims=True) + EPS) * ng_ref[...]
    ho_ref[...] = h.astype(ho_ref.dtype)


def _norm_residual(z, x, post_g, next_g):
    m, d = x.shape
    tm = _row_tile(m, 416)
    row = pl.BlockSpec((tm, d), lambda i: (i, 0))
    vec = pl.BlockSpec((1, d), lambda i: (0, 0))
    return pl.pallas_call(
        _norm_residual_kernel,
        grid=(m // tm,),
        in_specs=[row, row, vec, vec],
        out_specs=[row, row],
        out_shape=[jax.ShapeDtypeStruct((m, d), F32), jax.ShapeDtypeStruct((m, d), BF16)],
        compiler_params=_cparams(("parallel",)),
        name="norm_residual",
    )(z, x, post_g.reshape(1, d), next_g.reshape(1, d))


class _SideCast:
    def __init__(self, src, layer, steps):
        self.r, self.c = src.shape[1:]
        best = None
        for split in (1, 2, 4):
            if self.c % (split * LANE):
                continue
            for br in range(16, self.r + 1, 16):
                nblk = (self.r // br) * split
                if self.r % br == 0 and nblk <= steps and (best is None or nblk > best[0]):
                    best = (nblk, br, self.c // split)
        assert best is not None, (src.shape, steps)
        self.nblk, self.br, self.bc = best
        self.ncol = self.c // self.bc
        self.src, self.layer = src, layer

    def _block(self, step):
        b = jnp.minimum(step, self.nblk - 1)
        return b // self.ncol, b % self.ncol

    def in_spec(self, step_of):
        return pl.BlockSpec((None, self.br, self.bc), lambda *g: (self.layer,) + self._block(step_of(*g)))

    def out_spec(self, step_of):
        return pl.BlockSpec((self.br, self.bc), lambda *g: self._block(step_of(*g)))

    def out_shape(self):
        return jax.ShapeDtypeStruct((self.r, self.c), BF16)


def _side_cast(step, nblk, src_ref, dst_ref, scale=1.0):
    @pl.when(step < nblk)
    def _():
        dst_ref[...] = (src_ref[...] * scale).astype(dst_ref.dtype)


def _matmul_kernel(x_ref, w_ref, o_ref):
    o_ref[...] = jnp.dot(x_ref[...], w_ref[...], preferred_element_type=F32).astype(o_ref.dtype)


def _matmul_cast_kernel(x_ref, w_ref, src_ref, o_ref, dst_ref, *, nj, nblk, scale):
    o_ref[...] = jnp.dot(x_ref[...], w_ref[...], preferred_element_type=F32).astype(o_ref.dtype)
    _side_cast(pl.program_id(0) * nj + pl.program_id(1), nblk, src_ref, dst_ref, scale)


def _matmul(x, w, layer, tm, tn, out_dtype, name, cast=None, cast_scale=1.0):
    m, k = x.shape
    n = w.shape[-1]
    tm = _row_tile(m, tm)
    assert n % tn == 0, (n, tn)
    nj = n // tn
    w_spec = (pl.BlockSpec((k, tn), lambda i, j: (0, j)) if layer is None
              else pl.BlockSpec((None, k, tn), lambda i, j: (layer, 0, j)))
    x_spec = pl.BlockSpec((tm, k), lambda i, j: (i, 0))
    o_spec = pl.BlockSpec((tm, tn), lambda i, j: (i, j))
    o_shape = jax.ShapeDtypeStruct((m, n), out_dtype)
    if cast is None:
        return pl.pallas_call(
            _matmul_kernel, grid=(m // tm, nj), in_specs=[x_spec, w_spec], out_specs=o_spec,
            out_shape=o_shape, compiler_params=_cparams(("parallel", "parallel")), name=name,
        )(x, w)
    side = _SideCast(cast[0], cast[1], (m // tm) * nj)
    step_of = lambda i, j: i * nj + j
    x_spec = pl.BlockSpec((tm, k), lambda i, j: (i, 0), pipeline_mode=pl.Buffered(1))
    return pl.pallas_call(
        functools.partial(_matmul_cast_kernel, nj=nj, nblk=side.nblk, scale=cast_scale),
        grid=(m // tm, nj),
        in_specs=[x_spec, w_spec, side.in_spec(step_of)],
        out_specs=[o_spec, side.out_spec(step_of)],
        out_shape=[o_shape, side.out_shape()],
        compiler_params=_cparams(("arbitrary", "arbitrary")),
        name=name,
    )(x, w, side.src)


def _gate_merge_kernel(h_ref, o_ref, wg_ref, wb_ref, src_ref, out_ref, dst_ref, acc_ref, *, nj, nblk):
    n = pl.program_id(2)
    t = jnp.tanh(jnp.dot(h_ref[...], wg_ref[...], preferred_element_type=F32))
    y = jnp.dot(o_ref[...], wb_ref[...], preferred_element_type=F32)
    gated = y + y * t

    @pl.when(n == 0)
    def _():
        acc_ref[...] = gated

    @pl.when(n > 0)
    def _():
        acc_ref[...] += gated

    @pl.when(n == N_BRANCH - 1)
    def _():
        out_ref[...] = acc_ref[...].astype(out_ref.dtype)

    _side_cast((pl.program_id(0) * nj + pl.program_id(1)) * N_BRANCH + n, nblk, src_ref, dst_ref)


def _gate_merge(h, o4, w_gate, w_branch, cast, tn=1024):
    m, d = h.shape
    tm = _row_tile(m, 1040)
    nj = d // tn
    layer = cast[1]
    side = _SideCast(cast[0], layer, (m // tm) * nj * N_BRANCH)
    step_of = lambda i, j, n: (i * nj + j) * N_BRANCH + n
    return pl.pallas_call(
        functools.partial(_gate_merge_kernel, nj=nj, nblk=side.nblk),
        grid=(m // tm, nj, N_BRANCH),
        in_specs=[pl.BlockSpec((tm, d), lambda i, j, n: (i, 0), pipeline_mode=pl.Buffered(1)),
                  pl.BlockSpec((None, tm, MIX_W), lambda i, j, n: (n, i, 0)),
                  pl.BlockSpec((d, tn), lambda i, j, n: (0, n * nj + j)),
                  pl.BlockSpec((None, None, MIX_W, tn), lambda i, j, n: (layer, n, 0, j)),
                  side.in_spec(step_of)],
        out_specs=[pl.BlockSpec((tm, tn), lambda i, j, n: (i, j)), side.out_spec(step_of)],
        out_shape=[jax.ShapeDtypeStruct((m, d), BF16), side.out_shape()],
        scratch_shapes=[pltpu.VMEM((tm, tn), F32)],
        compiler_params=_cparams(("arbitrary", "arbitrary", "arbitrary")),
        name="gate_merge",
    )(h, o4, w_gate, w_branch, side.src)


def _ffn_up_kernel(h_ref, halo_ref, wa_ref, wb_ref, ca_ref, cb_ref, src_ref, o_ref, dst_ref,
                   *, tiles_per_seq, nj, nblk):
    i = pl.program_id(0)
    first = (i % tiles_per_seq) == 0
    h = h_ref[...]
    halo = halo_ref[...]
    row8 = lax.broadcasted_iota(jnp.int32, (8, 1), 0)

    def conv(w_ref, c_ref):
        u = jnp.dot(h, w_ref[...], preferred_element_type=F32)
        uh = jnp.dot(halo, w_ref[...], preferred_element_type=F32)
        uh = jnp.where(first, 0.0, uh)
        p1, p2 = uh[15:16], uh[14:15]
        u1 = pltpu.roll(u, 1, 0)
        u2 = pltpu.roll(u, 2, 0)
        c = c_ref[...]
        y = c[2:3] * u + c[1:2] * u1 + c[0:1] * u2
        f1 = jnp.where(row8 == 0, p1, u1[:8])
        f2 = jnp.where(row8 == 0, p2, jnp.where(row8 == 1, p1, u2[:8]))
        top = c[2:3] * u[:8] + c[1:2] * f1 + c[0:1] * f2
        return jnp.concatenate([top, y[8:]], axis=0)

    a = conv(wa_ref, ca_ref)
    a = a + a * jnp.tanh(a)
    b = conv(wb_ref, cb_ref)
    o_ref[...] = (a * b).astype(o_ref.dtype)
    _side_cast(i * nj + pl.program_id(1), nblk, src_ref, dst_ref)


def _ffn_up(h, w_up, conv_w, lp, cast, tn=256):
    m, d = h.shape
    tm = _row_tile(lp, 2080)
    nj = D_FF // tn
    halo_blk = tm // 16
    side = _SideCast(cast[0], cast[1], (m // tm) * nj)
    step_of = lambda i, j: i * nj + j
    layer = cast[1]
    kern = functools.partial(_ffn_up_kernel, tiles_per_seq=lp // tm, nj=nj, nblk=side.nblk)
    return pl.pallas_call(
        kern,
        grid=(m // tm, nj),
        in_specs=[pl.BlockSpec((tm, d), lambda i, j: (i, 0), pipeline_mode=pl.Buffered(1)),
                  pl.BlockSpec((16, d), lambda i, j: (jnp.maximum(i * halo_blk - 1, 0), 0)),
                  pl.BlockSpec((d, tn), lambda i, j: (0, j)),
                  pl.BlockSpec((d, tn), lambda i, j: (0, nj + j)),
                  pl.BlockSpec((None, FFN_CONV, tn), lambda i, j: (layer, 0, j)),
                  pl.BlockSpec((None, FFN_CONV, tn), lambda i, j: (layer, 0, nj + j)),
                  side.in_spec(step_of)],
        out_specs=[pl.BlockSpec((tm, tn), lambda i, j: (i, j)), side.out_spec(step_of)],
        out_shape=[jax.ShapeDtypeStruct((m, D_FF), BF16), side.out_shape()],
        compiler_params=_cparams(("arbitrary", "arbitrary")),
        name="ffn_up",
    )(h, h, w_up, w_up, conv_w, conv_w, side.src)


def _retention_kernel(q_ref, k_ref, v_ref, g_ref, cos_ref, sin_ref, lg_ref, gain_ref, o_ref, s_ref):
    r = pl.program_id(2)

    @pl.when(r == 0)
    def _():
        s_ref[...] = jnp.zeros_like(s_ref)

    half = RET_D // 2
    cos, sin = cos_ref[...], sin_ref[...]

    def rope(t):
        t1, t2 = t[:, :half], t[:, half:]
        return jnp.concatenate([t1 * cos - t2 * sin, t1 * sin + t2 * cos], axis=1)

    q = rope(q_ref[...]) * RET_D ** -0.5
    k = rope(k_ref[...])
    v = v_ref[...]
    n = q.shape[0]
    lg = lg_ref[...][:, :1]
    ri = lax.broadcasted_iota(jnp.int32, (n, n), 0)
    ci = lax.broadcasted_iota(jnp.int32, (n, n), 1)
    rel = (ri - ci).astype(F32)
    decay = jnp.where(rel >= 0, jnp.exp(lg * jnp.maximum(rel, 0.0)), 0.0)
    idx = lax.broadcasted_iota(jnp.int32, (n, 1), 0).astype(F32)
    q_decay = jnp.exp(lg * (idx + 1.0))
    k_decay = jnp.exp(lg * (n - 1.0 - idx))
    s = s_ref[...]
    scores = _bdot_nt(q, k) * decay
    o = _bdot(scores, v) + _bdot(q, s) * q_decay
    s_ref[...] = s * jnp.exp(lg * float(n)) + _bdot_tn(k * k_decay, v)
    o = o - jnp.mean(o, axis=-1, keepdims=True)
    o = o * lax.rsqrt(jnp.mean(o * o, axis=-1, keepdims=True) + EPS) * gain_ref[...]
    o_ref[...] = (o * _silu(g_ref[...])).astype(o_ref.dtype)


def _retention(proj, cos, sin, norm_gain, batch, lp):
    m = proj.shape[0]
    nr = lp // ROWS
    gamma = 1.0 - 2.0 ** (-5.0 - np.arange(RET_HEADS, dtype=np.float64))
    lg = jnp.asarray(np.broadcast_to(np.log(gamma)[:, None, None], (RET_HEADS, 1, LANE)), F32)

    def part(p):
        return pl.BlockSpec((ROWS, RET_D), lambda b, h, r: (b * nr + r, p * RET_HEADS + h))

    tab = pl.BlockSpec((ROWS, RET_D // 2), lambda b, h, r: (r, 0))
    return pl.pallas_call(
        _retention_kernel,
        grid=(batch, RET_HEADS, nr),
        in_specs=[part(0), part(1), part(2), part(3), tab, tab,
                  pl.BlockSpec((None, 1, LANE), lambda b, h, r: (h, 0, 0)),
                  pl.BlockSpec((1, RET_D), lambda b, h, r: (0, h))],
        out_specs=pl.BlockSpec((ROWS, RET_D), lambda b, h, r: (b * nr + r, h)),
        out_shape=jax.ShapeDtypeStruct((m, MIX_W), BF16),
        scratch_shapes=[pltpu.VMEM((RET_D, RET_D), F32)],
        compiler_params=_cparams(("parallel", "parallel", "arbitrary")),
        name="retention",
    )(proj, proj, proj, proj, cos, sin, lg, norm_gain.reshape(1, MIX_W))


GLA_LEVELS = 6


def _gla_consts():
    i, j = _np_masks()
    mats = [(j <= i), (j > i)]
    pair = []
    for lvl in range(GLA_LEVELS):
        s = 32 >> lvl
        blk = (i // (2 * s)) * (2 * s)
        mid = blk + s - 1
        second = (i - blk) >= s
        mats.append(second & (j > mid) & (j <= i))
        mats.append((~second) & (j > i) & (j <= mid))
        jb = (j // (2 * s)) * (2 * s)
        pair.append(second & (jb == blk) & ((j - jb) < s))
    pair.append(i == j)
    em = np.concatenate(mats, axis=0).astype(np.float32)
    pm = np.stack(pair, axis=0).astype(np.float32)
    return jnp.asarray(em, BF16), jnp.asarray(pm, F32)


def _gla_kernel(q_ref, k_ref, v_ref, g_ref, lr_ref, w2_ref, bias_ref, gain_ref, em_ref, pm_ref,
                o_ref, s_ref):
    @pl.when(pl.program_id(1) == 0)
    def _():
        s_ref[...] = jnp.zeros_like(s_ref)

    heads = range(GLA_HEADS)
    ks = [slice(h * GLA_DK, (h + 1) * GLA_DK) for h in heads]
    vs = [slice(h * GLA_DV, (h + 1) * GLA_DV) for h in heads]
    x = _bdot(lr_ref[...], w2_ref[...]) + bias_ref[...]
    la = (jnp.minimum(x, 0.0) - jnp.log1p(jnp.exp(-jnp.abs(x)))) / GLA_TAU
    ex = jnp.exp(_mask_dot(em_ref[...], la))
    e_b = ex[0:CH]
    e_rest = ex[CH:2 * CH]
    q = q_ref[...] * GLA_DK ** -0.5
    k = k_ref[...]
    v = [v_ref[:, s] for s in vs]
    scores = [_bdot_nt(q[:, s], k[:, s]) * pm_ref[GLA_LEVELS] for s in ks]
    for lvl in range(GLA_LEVELS):
        qe = q * ex[(2 + 2 * lvl) * CH:(3 + 2 * lvl) * CH]
        ke = k * ex[(3 + 2 * lvl) * CH:(4 + 2 * lvl) * CH]
        pm = pm_ref[lvl]
        scores = [scores[h] + _bdot_nt(qe[:, ks[h]], ke[:, ks[h]]) * pm for h in heads]
    st = [s_ref[h] for h in heads]
    qb = q * e_b
    kr = k * e_rest
    o = [_bdot(scores[h], v[h]) + _bdot_nt(qb[:, ks[h]], st[h]) for h in heads]
    upd = [_bdot_tn(v[h], kr[:, ks[h]]) for h in heads]
    for h in heads:
        s_ref[h] = st[h] * e_b[CH - 1:CH, ks[h]] + upd[h]
        oh = o[h] * lax.rsqrt(jnp.mean(o[h] * o[h], axis=-1, keepdims=True) + EPS) * gain_ref[:, vs[h]]
        o_ref[:, vs[h]] = (oh * _silu(g_ref[:, vs[h]])).astype(o_ref.dtype)


def _gla(proj, w2p, bias, norm_gain, batch, lp):
    m = proj.shape[0]
    nc = lp // CH
    em, pm = _gla_consts()
    qk_w = GLA_HEADS * GLA_DK
    const = lambda shape: pl.BlockSpec(shape, lambda b, c: (0,) * len(shape))
    return pl.pallas_call(
        _gla_kernel,
        grid=(batch, nc),
        in_specs=[pl.BlockSpec((CH, qk_w), lambda b, c: (b * nc + c, COL_GLA_QK // qk_w)),
                  pl.BlockSpec((CH, qk_w), lambda b, c: (b * nc + c, COL_GLA_QK // qk_w + 1)),
                  pl.BlockSpec((CH, MIX_W), lambda b, c: (b * nc + c, COL_GLA_VG // MIX_W)),
                  pl.BlockSpec((CH, MIX_W), lambda b, c: (b * nc + c, COL_GLA_VG // MIX_W + 1)),
                  pl.BlockSpec((CH, LANE), lambda b, c: (b * nc + c, COL_GLA_LR // LANE)),
                  const((LANE, qk_w)), const((1, qk_w)), const((1, MIX_W)),
                  const(em.shape), const(pm.shape)],
        out_specs=pl.BlockSpec((CH, MIX_W), lambda b, c: (b * nc + c, 0)),
        out_shape=jax.ShapeDtypeStruct((m, MIX_W), BF16),
        scratch_shapes=[pltpu.VMEM((GLA_HEADS, GLA_DV, GLA_DK), F32)],
        compiler_params=_cparams(("parallel", "arbitrary")),
        name="gla",
    )(proj, proj, proj, proj, proj, w2p, bias.reshape(1, -1), norm_gain.reshape(1, MIX_W), em, pm)


def _gdn_kernel(q_ref, k_ref, v_ref, z_ref, ab_ref, cq_ref, ck_ref, cv_ref, alog_ref, dtb_ref,
                gain_ref, tri_ref, blk_ref, o_ref, s_ref, carry_ref):
    @pl.when(pl.program_id(1) == 0)
    def _():
        s_ref[...] = jnp.zeros_like(s_ref)
        carry_ref[...] = jnp.zeros_like(carry_ref)

    def conv_silu(x_ref, c_ref, slot):
        x = x_ref[...]
        ext = jnp.concatenate([carry_ref[slot], x], axis=0)
        carry_ref[slot] = x[CH - 8:, :]
        c = c_ref[...]
        y = c[3:4] * x
        for t in range(1, GDN_CONV):
            y = y + c[3 - t:4 - t] * ext[8 - t:8 - t + CH, :]
        return _silu(y)

    def l2norm(t):
        return t * lax.rsqrt(jnp.sum(t * t, axis=-1, keepdims=True) + EPS)

    qc = conv_silu(q_ref, cq_ref, 0)
    kc = conv_silu(k_ref, ck_ref, 1)
    vc = conv_silu(v_ref, cv_ref, 2)
    ab = ab_ref[...]
    g_t = -jnp.exp(alog_ref[...]) * _softplus(ab + dtb_ref[...])
    beta_t = _sigmoid(ab)
    rep = lambda t, idx: jnp.broadcast_to(t[:, idx:idx + 1], (CH, LANE))
    heads = range(GDN_HEADS)
    hs = [slice(h * GDN_D, (h + 1) * GDN_D) for h in heads]
    gcum_all = _mask_dot(tri_ref[...], jnp.concatenate([rep(g_t, h) for h in heads], axis=1))

    same_blk = blk_ref[...]
    ri = lax.broadcasted_iota(jnp.int32, (CH, CH), 0)
    ci = lax.broadcasted_iota(jnp.int32, (CH, CH), 1)
    eye = (ri == ci).astype(F32)
    q = [l2norm(qc[:, s]) * GDN_D ** -0.5 for s in hs]
    k = [l2norm(kc[:, s]) for s in hs]
    v = [vc[:, s] for s in hs]
    beta = [rep(beta_t, GDN_HEADS + h) for h in heads]
    gcum = [gcum_all[:, s] for s in hs]
    grow = [jnp.transpose(g)[:1, :CH] for g in gcum]
    decay = [jnp.where(ri >= ci, jnp.exp(jnp.minimum(gcum[h][:, :1] - grow[h], 0.0)), 0.0) for h in heads]
    kb = [k[h] * beta[h] for h in heads]
    kq = [_bdot_nt(jnp.concatenate([kb[h], q[h]], axis=0), k[h]) for h in heads]
    a_mat = [jnp.where(ri > ci, kq[h][:CH] * decay[h], 0.0) for h in heads]
    attn = [kq[h][CH:] * decay[h] for h in heads]
    t_inv = _inv_unit_lower(a_mat, same_blk, eye)
    e_g = [jnp.exp(g) for g in gcum]
    sol = [_bdot(t_inv[h], jnp.concatenate([v[h] * beta[h], kb[h] * e_g[h]], axis=1)) for h in heads]
    s = [s_ref[h] for h in heads]
    m = [_bdot(jnp.concatenate([sol[h][:, GDN_D:], q[h] * e_g[h]], axis=0), s[h]) for h in heads]
    v_new = [sol[h][:, :GDN_D] - m[h][:CH] for h in heads]
    o = [m[h][CH:] + _bdot(attn[h], v_new[h]) for h in heads]
    g_last = [g[CH - 1:CH, :] for g in gcum]
    upd = [_bdot_tn(k[h] * jnp.exp(g_last[h] - gcum[h]), v_new[h]) for h in heads]
    gain = gain_ref[...]
    for h in heads:
        s_ref[h] = s[h] * jnp.exp(g_last[h]) + upd[h]
        oh = o[h] * lax.rsqrt(jnp.mean(o[h] * o[h], axis=-1, keepdims=True) + EPS) * gain
        o_ref[:, hs[h]] = (oh * _silu(z_ref[:, hs[h]])).astype(o_ref.dtype)


def _gdn(proj, conv_w, a_log, dt_bias, norm_gain, batch, lp):
    m = proj.shape[0]
    nc = lp // CH
    nh = GDN_HEADS
    pad = lambda t: jnp.pad(t.reshape(1, nh), ((0, 0), (0, LANE - nh)))
    part = lambda p: pl.BlockSpec((CH, MIX_W), lambda b, c: (b * nc + c, COL_GDN // MIX_W + p))
    cpart = lambda p: pl.BlockSpec((GDN_CONV, MIX_W), lambda b, c: (0, p))
    const = lambda shape: pl.BlockSpec(shape, lambda b, c: (0,) * len(shape))
    return pl.pallas_call(
        _gdn_kernel,
        grid=(batch, nc),
        in_specs=[part(0), part(1), part(2), part(3),
                  pl.BlockSpec((CH, LANE), lambda b, c: (b * nc + c, COL_GDN_AB // LANE)),
                  cpart(0), cpart(1), cpart(2),
                  const((1, LANE)), const((1, LANE)), const((1, GDN_D)),
                  const((CH, CH)), const((CH, CH))],
        out_specs=pl.BlockSpec((CH, MIX_W), lambda b, c: (b * nc + c, 0)),
        out_shape=jax.ShapeDtypeStruct((m, MIX_W), BF16),
        scratch_shapes=[pltpu.VMEM((nh, GDN_D, GDN_D), F32), pltpu.VMEM((3, 8, MIX_W), F32)],
        compiler_params=_cparams(("parallel", "arbitrary")),
        name="gdn",
    )(proj, proj, proj, proj, proj, conv_w, conv_w, conv_w, pad(a_log), pad(dt_bias),
      norm_gain.reshape(1, GDN_D), _const_tri(), _const_same_blk())


def _rwkv_prep_kernel(sr_ref, sk_ref, sv_ref, swa_ref, sg_ref, mu_ref, w0_ref, a0_ref, kk_ref, ka_ref,
                      w2_ref, a2_ref, g2_ref, seg_ref, segt_ref,
                      r_ref, lw_ref, k_ref, v_ref, na_ref, bb_ref, g_ref, carry_ref):
    rb = pl.program_id(1)

    @pl.when(rb == 0)
    def _():
        carry_ref[...] = jnp.zeros_like(carry_ref)

    x = jnp.concatenate([sr_ref[...], sk_ref[...], sv_ref[...], swa_ref[...], sg_ref[...]], axis=1)
    ext = jnp.concatenate([carry_ref[...], x], axis=0)
    carry_ref[...] = x[ROWS - 8:, :]
    prev = ext[7:7 + ROWS, :]
    s = x + (prev - x) * mu_ref[...]
    r = s[:, :MIX_W]
    k = s[:, MIX_W:2 * MIX_W]
    v = s[:, 2 * MIX_W:3 * MIX_W]
    wa_src = s[:, RWKV_LR:RWKV_LR + LANE]
    g_src = s[:, RWKV_LR + LANE:]
    lw_ref[...] = -RWKV_DECAY_SCALE * _sigmoid(w0_ref[...] + _bdot(jnp.tanh(wa_src), w2_ref[...]))
    a = _sigmoid(a0_ref[...] + _bdot(wa_src, a2_ref[...]))
    g_ref[...] = _bdot(_sigmoid(g_src), g2_ref[...])
    kk = k * kk_ref[...]
    sq = kk * kk
    sq_hi = sq.astype(BF16)
    sq_lo = (sq - sq_hi.astype(F32)).astype(BF16)
    seg = seg_ref[...]
    hsum = (jnp.dot(sq_hi, seg, preferred_element_type=F32)
            + jnp.dot(sq_lo, seg, preferred_element_type=F32))
    inv = lax.rsqrt(hsum + EPS)
    inv_hi = inv.astype(BF16)
    inv_lo = (inv - inv_hi.astype(F32)).astype(BF16)
    segt = segt_ref[...]
    inv_full = (jnp.dot(inv_hi, segt, preferred_element_type=F32)
                + jnp.dot(inv_lo, segt, preferred_element_type=F32))
    kk = kk * inv_full
    r_ref[...] = r
    k_ref[...] = k * (1.0 + (a - 1.0) * ka_ref[...])
    v_ref[...] = v
    na_ref[...] = -kk
    bb_ref[...] = kk * a


def _rwkv_scan_kernel(r_ref, lw_ref, k_ref, v_ref, na_ref, bb_ref, g_ref, rk_ref, lnw_ref, lnb_ref,
                      cm_ref, blk_ref, o_ref, h_ref):
    @pl.when(pl.program_id(1) == 0)
    def _():
        h_ref[...] = jnp.zeros_like(h_ref)

    n = RWKV_N
    npair = MIX_W // LANE
    ri = lax.broadcasted_iota(jnp.int32, (CH, CH), 0)
    ci = lax.broadcasted_iota(jnp.int32, (CH, CH), 1)
    eye = (ri == ci).astype(F32)
    strict = ri > ci
    ri2 = lax.broadcasted_iota(jnp.int32, (CH, LANE), 0)
    ci2 = lax.broadcasted_iota(jnp.int32, (CH, LANE), 1) % CH
    incl2 = ri2 >= ci2
    lo = lax.broadcasted_iota(jnp.int32, (1, LANE), 1) < n
    vi = lax.broadcasted_iota(jnp.int32, (LANE, LANE), 0) // n
    ki = lax.broadcasted_iota(jnp.int32, (LANE, LANE), 1) // n
    same_head = (vi == ki).astype(F32)
    same_blk = blk_ref[...]

    r, lw, k, v = r_ref[...], lw_ref[...], k_ref[...], v_ref[...]
    na, bb = na_ref[...], bb_ref[...]
    cums = _mask_dot(cm_ref[...], lw)
    cum, rest = cums[:CH], cums[CH:]
    e_cum = jnp.exp(cum)
    e_neg = jnp.exp(-cum)
    e_rest = jnp.exp(rest)
    rt = r * e_cum
    at = na * jnp.exp(cum - lw)
    bt = bb * e_neg
    kt = k * e_neg
    bh = bb * e_rest
    kh = k * e_rest

    pairs = [slice(p * LANE, (p + 1) * LANE) for p in range(npair)]
    pm = []
    for ps in pairs:
        a_p, r_p = at[:, ps], rt[:, ps]
        lhs = jnp.concatenate([jnp.where(lo, a_p, 0.0), jnp.where(lo, r_p, 0.0),
                               jnp.where(lo, 0.0, a_p), jnp.where(lo, 0.0, r_p)], axis=0)
        pm.append(_bdot_nt(lhs, jnp.concatenate([bt[:, ps], kt[:, ps]], axis=0)))
    heads = [(p, s) for p in range(npair) for s in range(2)]
    a_ab = [jnp.where(strict, pm[p][2 * s * CH:(2 * s + 1) * CH, :CH], 0.0) for p, s in heads]
    a_ak = [jnp.where(strict, pm[p][2 * s * CH:(2 * s + 1) * CH, CH:], 0.0) for p, s in heads]
    a_rbk = [jnp.where(incl2, pm[p][(2 * s + 1) * CH:(2 * s + 2) * CH, :], 0.0) for p, s in heads]
    t_inv = _inv_unit_lower([-a for a in a_ab], same_blk, eye)
    akv = [_bdot(a_ak[i], v[:, pairs[p]]) for i, (p, s) in enumerate(heads)]
    sol = [_bdot(t_inv[i], jnp.concatenate([at[:, pairs[p]], akv[i]], axis=1))
           for i, (p, s) in enumerate(heads)]
    sol = [jnp.where(jnp.concatenate([lo, lo], axis=1), sol[2 * p], sol[2 * p + 1]) for p in range(npair)]
    ht = [h_ref[p] for p in range(npair)]
    m = [_bdot_nt(jnp.concatenate([sol[p][:, :LANE], rt[:, pairs[p]]], axis=0), ht[p]) for p in range(npair)]
    uv = [jnp.concatenate([m[p][:CH] + sol[p][:, LANE:], v[:, pairs[p]]], axis=0) for p in range(npair)]
    yy = [_bdot(a_rbk[i], uv[p]) for i, (p, s) in enumerate(heads)]
    upd = [_bdot_tn(uv[p], jnp.concatenate([bh[:, pairs[p]], kh[:, pairs[p]]], axis=0)) for p in range(npair)]
    for p, ps in enumerate(pairs):
        h_ref[p] = ht[p] * e_cum[CH - 1:CH, ps] + upd[p] * same_head
        y = m[p][CH:] + jnp.where(lo, yy[2 * p], yy[2 * p + 1])

        def head_sum(t):
            s0 = jnp.sum(jnp.where(lo, t, 0.0), axis=-1, keepdims=True)
            s1 = jnp.sum(jnp.where(lo, 0.0, t), axis=-1, keepdims=True)
            return jnp.where(lo, s0, s1)

        y = y - head_sum(y) * (1.0 / n)
        y = y * lax.rsqrt(head_sum(y * y) * (1.0 / n) + RWKV_GN_EPS)
        y = y * lnw_ref[:, ps] + lnb_ref[:, ps]
        bonus = head_sum(r[:, ps] * k[:, ps] * rk_ref[:, ps])
        o_ref[:, ps] = ((y + bonus * v[:, ps]) * g_ref[:, ps]).astype(o_ref.dtype)


def _rwkv(proj, mu_p, w0, w2p, a0, a2p, g2p, kk, ka, rk, ln_w, ln_b, batch, lp):
    m = proj.shape[0]
    nr = lp // ROWS
    c = np.arange(MIX_W)[:, None] // RWKV_N == np.arange(LANE)[None, :]
    seg = jnp.asarray(c.astype(np.float32), BF16)
    segt = jnp.asarray(c.T.astype(np.float32), BF16)
    vec = lambda t: t.reshape(1, -1)
    const2 = lambda shape: pl.BlockSpec(shape, lambda b, r: (0, 0))
    row_w = pl.BlockSpec((ROWS, MIX_W), lambda b, r: (b * nr + r, 0))
    wide = jax.ShapeDtypeStruct((m, MIX_W), F32)
    src = lambda width, col: pl.BlockSpec((ROWS, width), lambda b, r: (b * nr + r, col // width))
    r_, lw, k_, v_, na, bb, g = pl.pallas_call(
        _rwkv_prep_kernel,
        grid=(batch, nr),
        in_specs=[src(MIX_W, COL_RWKV), src(MIX_W, COL_RWKV + MIX_W), src(MIX_W, COL_RWKV + 2 * MIX_W),
                  src(LANE, COL_RWKV_WA), src(2 * LANE, COL_RWKV_G),
                  const2((1, RWKV_PW)), const2((1, MIX_W)), const2((1, MIX_W)), const2((1, MIX_W)),
                  const2((1, MIX_W)), const2((LANE, MIX_W)), const2((LANE, MIX_W)),
                  const2((2 * LANE, MIX_W)), const2((MIX_W, LANE)), const2((LANE, MIX_W))],
        out_specs=[row_w] * 7,
        out_shape=[wide] * 7,
        scratch_shapes=[pltpu.VMEM((8, RWKV_PW), F32)],
        compiler_params=_cparams(("parallel", "arbitrary")),
        name="rwkv_prep",
    )(proj, proj, proj, proj, proj, vec(mu_p), vec(w0), vec(a0), vec(kk), vec(ka), w2p, a2p, g2p, seg, segt)

    i, j = _np_masks()
    cm = jnp.asarray(np.concatenate([(j <= i), (j > i)], axis=0).astype(np.float32), BF16)
    nc = lp // CH
    chunk = pl.BlockSpec((CH, MIX_W), lambda b, c: (b * nc + c, 0))
    return pl.pallas_call(
        _rwkv_scan_kernel,
        grid=(batch, nc),
        in_specs=[chunk] * 7 + [const2((1, MIX_W))] * 3 + [const2((2 * CH, CH)), const2((CH, CH))],
        out_specs=chunk,
        out_shape=jax.ShapeDtypeStruct((m, MIX_W), BF16),
        scratch_shapes=[pltpu.VMEM((MIX_W // LANE, LANE, LANE), F32)],
        compiler_params=_cparams(("parallel", "arbitrary")),
        name="rwkv_scan",
    )(r_, lw, k_, v_, na, bb, g, vec(rk), vec(ln_w), vec(ln_b), cm, _const_same_blk())


def _pad_cols(w, width):
    return jnp.pad(w, ((0, 0),) * (w.ndim - 1) + ((0, width - w.shape[-1]),))


def _pad_rows(w, rows, offset=0):
    return jnp.pad(w, ((offset, rows - offset - w.shape[0]), (0, 0)))


def _reorder_w_in(w_in):
    ret_w = 4 * MIX_W
    gla = ret_w
    gla_v = gla + 2 * GLA_HEADS * GLA_DK
    gla_lr = gla_v + 2 * GLA_HEADS * GLA_DV
    gdn = gla_lr + GLA_RANK
    gdn_ab = gdn + 4 * MIX_W
    rwkv = gdn_ab + 2 * GDN_HEADS
    rwkv_wa = rwkv + 3 * MIX_W
    rwkv_g = rwkv_wa + RWKV_W_RANK + RWKV_A_RANK
    w = w_in.astype(BF16)
    pieces = [w[..., :ret_w], w[..., gdn:gdn_ab], w[..., rwkv:rwkv_wa], w[..., gla_v:gla_lr],
              w[..., gla:gla_v], _pad_cols(w[..., gdn_ab:rwkv], LANE), w[..., rwkv_wa:rwkv_g],
              _pad_cols(w[..., rwkv_g:], 2 * LANE), _pad_cols(w[..., gla_lr:gdn], PROJ_W - COL_GLA_LR)]
    out = jnp.concatenate(pieces, axis=-1)
    assert out.shape[-1] == PROJ_W, out.shape
    return out


def _mixer_sublayer(x, h, lyr, big, layer, cos, sin, batch, lp, next_g):
    proj, w_gate_half = _matmul(h, big["w_proj"], layer, 1040, PROJ_TN, F32, "proj", (big["w_gate"], layer), 0.5)

    o_a = _retention(proj, cos, sin, lyr["ret_norm"], batch, lp)
    o_b = _gla(proj, _pad_rows(lyr["gla_w2"], LANE).astype(BF16), lyr["gla_b"], lyr["gla_norm"],
               batch, lp)
    o_c = _gdn(proj, lyr["gdn_conv"], lyr["gdn_a_log"], lyr["gdn_dt_bias"], lyr["gdn_norm"], batch, lp)
    mu_p = jnp.pad(lyr["rwkv_mu"], (0, RWKV_PW - lyr["rwkv_mu"].shape[0]))
    w2p = _pad_rows(lyr["rwkv_w2"], LANE).astype(BF16)
    a2p = _pad_rows(lyr["rwkv_a2"], LANE, RWKV_W_RANK).astype(BF16)
    g2p = _pad_rows(lyr["rwkv_g2"], 2 * LANE).astype(BF16)
    o_d = _rwkv(proj, mu_p, lyr["rwkv_w0"], w2p, lyr["rwkv_a0"], a2p, g2p, lyr["rwkv_kk"],
                lyr["rwkv_ka"], lyr["rwkv_rk"], lyr["rwkv_ln_w"], lyr["rwkv_ln_b"], batch, lp)

    o4 = jnp.stack([o_a, o_b, o_c, o_d], axis=0)
    merged, w_up = _gate_merge(h, o4, w_gate_half, big["w_branch"], (big["w_up"], layer))
    z = _matmul(merged, big["w_out"], layer, 1040, 1024, F32, "w_out")
    x, h = _norm_residual(z, x, lyr["post_mix"], next_g)
    return x, h, w_up


def _ffn_sublayer(x, h, w_up, lyr, big, layer, lp, next_g):
    act, w_down = _ffn_up(h, w_up, big["ffn_conv"], lp, (big["w_down"], layer))
    z = _matmul(act, w_down, None, 520, 512, F32, "ffn_down")
    return _norm_residual(z, x, lyr["post_ffn"], next_g)


_LAYER_KEYS = ("pre_mix", "w_in", "ret_norm", "gla_w2", "gla_b", "gla_norm", "gdn_conv", "gdn_a_log",
               "gdn_dt_bias", "gdn_norm", "rwkv_mu", "rwkv_w0", "rwkv_w2", "rwkv_a0", "rwkv_a2", "rwkv_g2",
               "rwkv_kk", "rwkv_ka", "rwkv_rk", "rwkv_ln_w", "rwkv_ln_b", "w_branch", "w_gate", "w_out",
               "post_mix", "pre_ffn", "w_up", "ffn_conv", "w_down", "post_ffn")
_BIG_KEYS = ("w_in", "w_gate", "w_branch", "w_out", "w_up", "w_down", "ffn_conv")


def _big_weights(params):
    big = {k: params[k] for k in ("w_gate", "w_up", "w_down")}
    big["w_branch"] = (0.5 * params["w_branch"]).astype(BF16)
    half_a = jnp.concatenate([jnp.full((D_FF,), 0.5, F32), jnp.ones((D_FF,), F32)])
    big["ffn_conv"] = params["ffn_conv"] * half_a
    big["w_out"] = params["w_out"].astype(BF16)
    big["w_proj"] = _reorder_w_in(params["w_in"])
    return big


def _trunk(x, meta, params):
    batch, seq, d = x.shape
    depth = params["pre_mix"].shape[0]
    l = N_META + seq
    lp = -(-l // ROWS) * ROWS
    hcat = jnp.concatenate([jnp.broadcast_to(meta.astype(x.dtype)[None], (batch, N_META, d)), x,
                            jnp.zeros((batch, lp - l, d), x.dtype)], axis=1)
    xr = hcat.reshape(batch * lp, d)

    half = RET_D // 2
    inv_freq = ROPE_BASE ** (-jnp.arange(half, dtype=F32) / half)
    ang = jnp.arange(lp, dtype=F32)[:, None] * inv_freq[None, :]
    cos, sin = jnp.cos(ang), jnp.sin(ang)

    big = _big_weights(params)
    layers = [{k: params[k][i] for k in _LAYER_KEYS if k not in _BIG_KEYS} for i in range(depth)]
    h = _prenorm(xr, layers[0]["pre_mix"])
    for i, lyr in enumerate(layers):
        xr, h, w_up = _mixer_sublayer(xr, h, lyr, big, i, cos, sin, batch, lp, lyr["pre_ffn"])
        next_g = layers[i + 1]["pre_mix"] if i + 1 < depth else lyr["pre_ffn"]
        xr, h = _ffn_sublayer(xr, h, w_up, lyr, big, i, lp, next_g)
    return xr.reshape(batch, lp, d)[:, N_META:l]


def kernel(x, meta, pre_mix, w_in, ret_norm, gla_w2, gla_b, gla_norm, gdn_conv, gdn_a_log, gdn_dt_bias,
           gdn_norm, rwkv_mu, rwkv_w0, rwkv_w2, rwkv_a0, rwkv_a2, rwkv_g2, rwkv_kk, rwkv_ka, rwkv_rk,
           rwkv_ln_w, rwkv_ln_b, w_branch, w_gate, w_out, post_mix, pre_ffn, w_up, ffn_conv, w_down, post_ffn):
    params = dict(zip(_LAYER_KEYS, (pre_mix, w_in, ret_norm, gla_w2, gla_b, gla_norm, gdn_conv, gdn_a_log,
                                    gdn_dt_bias, gdn_norm, rwkv_mu, rwkv_w0, rwkv_w2, rwkv_a0, rwkv_a2,
                                    rwkv_g2, rwkv_kk, rwkv_ka, rwkv_rk, rwkv_ln_w, rwkv_ln_b, w_branch,
                                    w_gate, w_out, post_mix, pre_ffn, w_up, ffn_conv, w_down, post_ffn)))
    return _trunk(x, meta, params)
```

```python
import functools

import numpy as np
import jax
import jax.numpy as jnp
from jax import lax
from jax.experimental import pallas as pl
from jax.experimental.pallas import tpu as pltpu

F32 = jnp.float32
BF16 = jnp.bfloat16

D_MODEL = 4096
N_META = 16
N_BRANCH = 4
MIX_W = D_MODEL // N_BRANCH
EPS = 1e-6
D_FF = 11008
FFN_CONV = 3

RET_HEADS, RET_D = 4, 256
ROPE_BASE = 10000.0
GLA_HEADS, GLA_DK, GLA_DV, GLA_RANK, GLA_TAU = 4, 128, 256, 16, 16.0
GDN_HEADS, GDN_D, GDN_CONV = 8, 128, 4
RWKV_HEADS, RWKV_N = 16, 64
RWKV_W_RANK, RWKV_A_RANK, RWKV_G_RANK = 64, 64, 160
RWKV_DECAY_SCALE = 0.606531
RWKV_GN_EPS = 64e-5

LANE = 128
CH = 64
SUB = 16
ROWS = 320
VMEM_LIMIT = 60 * 1024 * 1024

RWKV_LR = 3 * MIX_W
RWKV_PW = 3 * MIX_W + LANE + 2 * LANE
COL_RET = 0
COL_GDN = COL_RET + 4 * MIX_W
COL_RWKV = COL_GDN + 4 * MIX_W
COL_GLA_VG = COL_RWKV + 3 * MIX_W
COL_GLA_QK = COL_GLA_VG + 2 * MIX_W
COL_GDN_AB = COL_GLA_QK + 2 * GLA_HEADS * GLA_DK
COL_RWKV_WA = COL_GDN_AB + LANE
COL_RWKV_G = COL_RWKV_WA + LANE
COL_GLA_LR = COL_RWKV_G + 2 * LANE
PROJ_TN = 1024
PROJ_W = -(-(COL_GLA_LR + LANE) // PROJ_TN) * PROJ_TN


def _row_tile(m, pref):
    best = 16
    for t in range(16, min(m, pref) + 1, 16):
        if m % t == 0:
            best = t
    assert m % best == 0, (m, pref)
    return best


def _cparams(sem):
    return pltpu.CompilerParams(dimension_semantics=sem, vmem_limit_bytes=VMEM_LIMIT)


def _bdot(a, b):
    return jnp.dot(a.astype(BF16), b.astype(BF16), preferred_element_type=F32)


def _bdot_nt(a, b):
    return lax.dot_general(a.astype(BF16), b.astype(BF16), (((1,), (1,)), ((), ())),
                           preferred_element_type=F32)


def _bdot_tn(a, b):
    return lax.dot_general(a.astype(BF16), b.astype(BF16), (((0,), (0,)), ((), ())),
                           preferred_element_type=F32)


def _mask_dot(mask, x):
    hi = x.astype(BF16)
    lo = (x - hi.astype(F32)).astype(BF16)
    return (jnp.dot(mask, hi, preferred_element_type=F32)
            + jnp.dot(mask, lo, preferred_element_type=F32))


def _sigmoid(x):
    return 0.5 + 0.5 * jnp.tanh(0.5 * x)


def _silu(x):
    return x * _sigmoid(x)


def _softplus(x):
    return jnp.maximum(x, 0.0) + jnp.log1p(jnp.exp(-jnp.abs(x)))


def _inv_unit_lower(mats, same_blk, eye):
    each = lambda f, *ls: [f(*t) for t in zip(*ls)]
    d = [a * same_blk for a in mats]
    l = each(lambda a, dd: a - dd, mats, d)
    d2 = each(_bdot, d, d)
    d4 = each(_bdot, d2, d2)
    d8 = each(_bdot, d4, d4)
    p = [eye - dd for dd in d]
    p = each(lambda pp, m: pp + _bdot(pp, m), p, d2)
    p = each(lambda pp, m: pp + _bdot(pp, m), p, d4)
    td = each(lambda pp, m: pp + _bdot(pp, m), p, d8)
    x = each(_bdot, td, l)
    x2 = each(_bdot, x, x)
    q = [eye - xx for xx in x]
    q = each(lambda qq, m: qq + _bdot(qq, m), q, x2)
    return each(_bdot, q, td)


def _np_masks():
    i = np.arange(CH)[:, None]
    j = np.arange(CH)[None, :]
    return i, j


def _const_tri():
    i, j = _np_masks()
    return jnp.asarray((j <= i).astype(np.float32), BF16)


def _const_same_blk():
    i, j = _np_masks()
    return jnp.asarray((i // SUB == j // SUB).astype(np.float32), F32)


def _prenorm_kernel(x_ref, g_ref, o_ref):
    x = x_ref[...]
    y = x * lax.rsqrt(jnp.mean(x * x, axis=-1, keepdims=True) + EPS)
    o_ref[...] = (y * g_ref[...]).astype(o_ref.dtype)


def _prenorm(x, g):
    m, d = x.shape
    tm = _row_tile(m, 416)
    return pl.pallas_call(
        _prenorm_kernel,
        grid=(m // tm,),
        in_specs=[pl.BlockSpec((tm, d), lambda i: (i, 0)),
                  pl.BlockSpec((1, d), lambda i: (0, 0))],
        out_specs=pl.BlockSpec((tm, d), lambda i: (i, 0)),
        out_shape=jax.ShapeDtypeStruct((m, d), BF16),
        compiler_params=_cparams(("parallel",)),
        name="prenorm",
    )(x, g.reshape(1, d))


def _norm_residual_kernel(z_ref, x_ref, pg_ref, ng_ref, xo_ref, ho_ref):
    z = z_ref[...]
    y = z * lax.rsqrt(jnp.mean(z * z, axis=-1, keepdims=True) + EPS) * pg_ref[...]
    xn = x_ref[...] + y
    xo_ref[...] = xn
    h = xn * lax.rsqrt(jnp.mean(xn * xn, axis=-1, keepdims=True) + EPS) * ng_ref[...]
    ho_ref[...] = h.astype(ho_ref.dtype)


def _residual_kernel(z_ref, x_ref, pg_ref, xo_ref):
    z = z_ref[...]
    xo_ref[...] = x_ref[...] + z * lax.rsqrt(jnp.mean(z * z, axis=-1, keepdims=True) + EPS) * pg_ref[...]


def _norm_residual(z, x, post_g, next_g):
    m, d = x.shape
    tm = _row_tile(m, 416)
    row = pl.BlockSpec((tm, d), lambda i: (i, 0))
    vec = pl.BlockSpec((1, d), lambda i: (0, 0))
    if next_g is None:
        return pl.pallas_call(
            _residual_kernel, grid=(m // tm,), in_specs=[row, row, vec], out_specs=row,
            out_shape=jax.ShapeDtypeStruct((m, d), F32), compiler_params=_cparams(("parallel",)),
            name="residual",
        )(z, x, post_g.reshape(1, d)), None
    return pl.pallas_call(
        _norm_residual_kernel,
        grid=(m // tm,),
        in_specs=[row, row, vec, vec],
        out_specs=[row, row],
        out_shape=[jax.ShapeDtypeStruct((m, d), F32), jax.ShapeDtypeStruct((m, d), BF16)],
        compiler_params=_cparams(("parallel",)),
        name="norm_residual",
    )(z, x, post_g.reshape(1, d), next_g.reshape(1, d))


class _SideCast:
    def __init__(self, src, layer, steps):
        self.r, self.c = src.shape[1:]
        best = None
        for split in (1, 2, 4):
            if self.c % (split * LANE):
                continue
            for br in range(16, self.r + 1, 16):
                nblk = (self.r // br) * split
                if self.r % br == 0 and nblk <= steps and (best is None or nblk > best[0]):
                    best = (nblk, br, self.c // split)
        assert best is not None, (src.shape, steps)
        self.nblk, self.br, self.bc = best
        self.ncol = self.c // self.bc
        self.src, self.layer = src, layer

    def _block(self, step):
        b = jnp.minimum(step, self.nblk - 1)
        return b // self.ncol, b % self.ncol

    def in_spec(self, step_of):
        return pl.BlockSpec((None, self.br, self.bc), lambda *g: (self.layer,) + self._block(step_of(*g)))

    def out_spec(self, step_of):
        return pl.BlockSpec((self.br, self.bc), lambda *g: self._block(step_of(*g)))

    def out_shape(self):
        return jax.ShapeDtypeStruct((self.r, self.c), BF16)


def _side_cast(step, nblk, src_ref, dst_ref, scale=1.0):
    @pl.when(step < nblk)
    def _():
        dst_ref[...] = (src_ref[...] * scale).astype(dst_ref.dtype)


def _matmul_kernel(x_ref, w_ref, o_ref):
    o_ref[...] = jnp.dot(x_ref[...], w_ref[...], preferred_element_type=F32).astype(o_ref.dtype)


def _matmul_cast_kernel(x_ref, w_ref, src_ref, o_ref, dst_ref, *, nj, nblk, scale):
    o_ref[...] = jnp.dot(x_ref[...], w_ref[...], preferred_element_type=F32).astype(o_ref.dtype)
    _side_cast(pl.program_id(0) * nj + pl.program_id(1), nblk, src_ref, dst_ref, scale)


def _matmul(x, w, layer, tm, tn, out_dtype, name, cast=None, cast_scale=1.0):
    m, k = x.shape
    n = w.shape[-1]
    tm = _row_tile(m, tm)
    assert n % tn == 0, (n, tn)
    nj = n // tn
    w_spec = (pl.BlockSpec((k, tn), lambda i, j: (0, j)) if layer is None
              else pl.BlockSpec((None, k, tn), lambda i, j: (layer, 0, j)))
    x_spec = pl.BlockSpec((tm, k), lambda i, j: (i, 0))
    o_spec = pl.BlockSpec((tm, tn), lambda i, j: (i, j))
    o_shape = jax.ShapeDtypeStruct((m, n), out_dtype)
    if cast is None:
        return pl.pallas_call(
            _matmul_kernel, grid=(m // tm, nj), in_specs=[x_spec, w_spec], out_specs=o_spec,
            out_shape=o_shape, compiler_params=_cparams(("parallel", "parallel")), name=name,
        )(x, w)
    side = _SideCast(cast[0], cast[1], (m // tm) * nj)
    step_of = lambda i, j: i * nj + j
    x_spec = pl.BlockSpec((tm, k), lambda i, j: (i, 0), pipeline_mode=pl.Buffered(1))
    return pl.pallas_call(
        functools.partial(_matmul_cast_kernel, nj=nj, nblk=side.nblk, scale=cast_scale),
        grid=(m // tm, nj),
        in_specs=[x_spec, w_spec, side.in_spec(step_of)],
        out_specs=[o_spec, side.out_spec(step_of)],
        out_shape=[o_shape, side.out_shape()],
        compiler_params=_cparams(("arbitrary", "arbitrary")),
        name=name,
    )(x, w, side.src)


def _gate_merge_kernel(h_ref, o_ref, wg_ref, wb_ref, src_ref, out_ref, dst_ref, acc_ref, *, nj, nblk):
    n = pl.program_id(2)
    t = jnp.tanh(jnp.dot(h_ref[...], wg_ref[...], preferred_element_type=F32))
    y = jnp.dot(o_ref[...], wb_ref[...], preferred_element_type=F32)
    gated = y + y * t

    @pl.when(n == 0)
    def _():
        acc_ref[...] = gated

    @pl.when(n > 0)
    def _():
        acc_ref[...] += gated

    @pl.when(n == N_BRANCH - 1)
    def _():
        out_ref[...] = acc_ref[...].astype(out_ref.dtype)

    _side_cast((pl.program_id(0) * nj + pl.program_id(1)) * N_BRANCH + n, nblk, src_ref, dst_ref)


def _gate_merge(h, o4, w_gate, w_branch, cast, tn=1024):
    m, d = h.shape
    tm = _row_tile(m, 1040)
    nj = d // tn
    layer = cast[1]
    side = _SideCast(cast[0], layer, (m // tm) * nj * N_BRANCH)
    step_of = lambda i, j, n: (i * nj + j) * N_BRANCH + n
    return pl.pallas_call(
        functools.partial(_gate_merge_kernel, nj=nj, nblk=side.nblk),
        grid=(m // tm, nj, N_BRANCH),
        in_specs=[pl.BlockSpec((tm, d), lambda i, j, n: (i, 0), pipeline_mode=pl.Buffered(1)),
                  pl.BlockSpec((None, tm, MIX_W), lambda i, j, n: (n, i, 0)),
                  pl.BlockSpec((d, tn), lambda i, j, n: (0, n * nj + j)),
                  pl.BlockSpec((None, None, MIX_W, tn), lambda i, j, n: (layer, n, 0, j)),
                  side.in_spec(step_of)],
        out_specs=[pl.BlockSpec((tm, tn), lambda i, j, n: (i, j)), side.out_spec(step_of)],
        out_shape=[jax.ShapeDtypeStruct((m, d), BF16), side.out_shape()],
        scratch_shapes=[pltpu.VMEM((tm, tn), F32)],
        compiler_params=_cparams(("arbitrary", "arbitrary", "arbitrary")),
        name="gate_merge",
    )(h, o4, w_gate, w_branch, side.src)


def _ffn_up_kernel(h_ref, halo_ref, wa_ref, wb_ref, ca_ref, cb_ref, src_ref, o_ref, dst_ref,
                   *, tiles_per_seq, nj, nblk):
    i = pl.program_id(0)
    first = (i % tiles_per_seq) == 0
    h = h_ref[...]
    halo = halo_ref[...]
    row8 = lax.broadcasted_iota(jnp.int32, (8, 1), 0)

    def conv(w_ref, c_ref):
        u = jnp.dot(h, w_ref[...], preferred_element_type=F32)
        uh = jnp.dot(halo, w_ref[...], preferred_element_type=F32)
        uh = jnp.where(first, 0.0, uh)
        p1, p2 = uh[15:16], uh[14:15]
        u1 = pltpu.roll(u, 1, 0)
        u2 = pltpu.roll(u, 2, 0)
        c = c_ref[...]
        y = c[2:3] * u + c[1:2] * u1 + c[0:1] * u2
        f1 = jnp.where(row8 == 0, p1, u1[:8])
        f2 = jnp.where(row8 == 0, p2, jnp.where(row8 == 1, p1, u2[:8]))
        top = c[2:3] * u[:8] + c[1:2] * f1 + c[0:1] * f2
        return jnp.concatenate([top, y[8:]], axis=0)

    a = conv(wa_ref, ca_ref)
    a = a + a * jnp.tanh(a)
    b = conv(wb_ref, cb_ref)
    o_ref[...] = (a * b).astype(o_ref.dtype)
    _side_cast(i * nj + pl.program_id(1), nblk, src_ref, dst_ref)


def _ffn_up(h, w_up, conv_w, lp, cast, tn=256):
    m, d = h.shape
    tm = _row_tile(lp, 2080)
    nj = D_FF // tn
    halo_blk = tm // 16
    side = _SideCast(cast[0], cast[1], (m // tm) * nj)
    step_of = lambda i, j: i * nj + j
    layer = cast[1]
    kern = functools.partial(_ffn_up_kernel, tiles_per_seq=lp // tm, nj=nj, nblk=side.nblk)
    return pl.pallas_call(
        kern,
        grid=(m // tm, nj),
        in_specs=[pl.BlockSpec((tm, d), lambda i, j: (i, 0), pipeline_mode=pl.Buffered(1)),
                  pl.BlockSpec((16, d), lambda i, j: (jnp.maximum(i * halo_blk - 1, 0), 0)),
                  pl.BlockSpec((d, tn), lambda i, j: (0, j)),
                  pl.BlockSpec((d, tn), lambda i, j: (0, nj + j)),
                  pl.BlockSpec((None, FFN_CONV, tn), lambda i, j: (layer, 0, j)),
                  pl.BlockSpec((None, FFN_CONV, tn), lambda i, j: (layer, 0, nj + j)),
                  side.in_spec(step_of)],
        out_specs=[pl.BlockSpec((tm, tn), lambda i, j: (i, j)), side.out_spec(step_of)],
        out_shape=[jax.ShapeDtypeStruct((m, D_FF), BF16), side.out_shape()],
        compiler_params=_cparams(("arbitrary", "arbitrary")),
        name="ffn_up",
    )(h, h, w_up, w_up, conv_w, conv_w, side.src)


def _retention_kernel(q_ref, k_ref, v_ref, g_ref, cos_ref, sin_ref, lg_ref, gain_ref, o_ref, s_ref):
    r = pl.program_id(2)

    @pl.when(r == 0)
    def _():
        s_ref[...] = jnp.zeros_like(s_ref)

    half = RET_D // 2
    cos, sin = cos_ref[...], sin_ref[...]

    def rope(t):
        t1, t2 = t[:, :half], t[:, half:]
        return jnp.concatenate([t1 * cos - t2 * sin, t1 * sin + t2 * cos], axis=1)

    q = rope(q_ref[...]) * RET_D ** -0.5
    k = rope(k_ref[...])
    v = v_ref[...]
    n = q.shape[0]
    lg = lg_ref[...][:, :1]
    ri = lax.broadcasted_iota(jnp.int32, (n, n), 0)
    ci = lax.broadcasted_iota(jnp.int32, (n, n), 1)
    rel = (ri - ci).astype(F32)
    decay = jnp.where(rel >= 0, jnp.exp(lg * jnp.maximum(rel, 0.0)), 0.0)
    idx = lax.broadcasted_iota(jnp.int32, (n, 1), 0).astype(F32)
    q_decay = jnp.exp(lg * (idx + 1.0))
    k_decay = jnp.exp(lg * (n - 1.0 - idx))
    s = s_ref[...]
    scores = _bdot_nt(q, k) * decay
    o = _bdot(scores, v) + _bdot(q, s) * q_decay
    s_ref[...] = s * jnp.exp(lg * float(n)) + _bdot_tn(k * k_decay, v)
    o = o - jnp.mean(o, axis=-1, keepdims=True)
    o = o * lax.rsqrt(jnp.mean(o * o, axis=-1, keepdims=True) + EPS) * gain_ref[...]
    o_ref[...] = (o * _silu(g_ref[...])).astype(o_ref.dtype)


def _retention(proj, cos, sin, norm_gain, batch, lp):
    m = proj.shape[0]
    nr = lp // ROWS
    gamma = 1.0 - 2.0 ** (-5.0 - np.arange(RET_HEADS, dtype=np.float64))
    lg = jnp.asarray(np.broadcast_to(np.log(gamma)[:, None, None], (RET_HEADS, 1, LANE)), F32)

    def part(p):
        return pl.BlockSpec((ROWS, RET_D), lambda b, h, r: (b * nr + r, p * RET_HEADS + h))

    tab = pl.BlockSpec((ROWS, RET_D // 2), lambda b, h, r: (r, 0))
    return pl.pallas_call(
        _retention_kernel,
        grid=(batch, RET_HEADS, nr),
        in_specs=[part(0), part(1), part(2), part(3), tab, tab,
                  pl.BlockSpec((None, 1, LANE), lambda b, h, r: (h, 0, 0)),
                  pl.BlockSpec((1, RET_D), lambda b, h, r: (0, h))],
        out_specs=pl.BlockSpec((ROWS, RET_D), lambda b, h, r: (b * nr + r, h)),
        out_shape=jax.ShapeDtypeStruct((m, MIX_W), BF16),
        scratch_shapes=[pltpu.VMEM((RET_D, RET_D), F32)],
        compiler_params=_cparams(("parallel", "parallel", "arbitrary")),
        name="retention",
    )(proj, proj, proj, proj, cos, sin, lg, norm_gain.reshape(1, MIX_W))


GLA_LEVELS = 6


def _gla_consts():
    i, j = _np_masks()
    mats = [(j <= i), (j > i)]
    pair = []
    for lvl in range(GLA_LEVELS):
        s = 32 >> lvl
        blk = (i // (2 * s)) * (2 * s)
        mid = blk + s - 1
        second = (i - blk) >= s
        mats.append(second & (j > mid) & (j <= i))
        mats.append((~second) & (j > i) & (j <= mid))
        jb = (j // (2 * s)) * (2 * s)
        pair.append(second & (jb == blk) & ((j - jb) < s))
    pair.append(i == j)
    em = np.concatenate(mats, axis=0).astype(np.float32)
    pm = np.stack(pair, axis=0).astype(np.float32)
    return jnp.asarray(em, BF16), jnp.asarray(pm, F32)


def _gla_kernel(q_ref, k_ref, v_ref, g_ref, lr_ref, w2_ref, bias_ref, gain_ref, em_ref, pm_ref,
                o_ref, s_ref):
    @pl.when(pl.program_id(1) == 0)
    def _():
        s_ref[...] = jnp.zeros_like(s_ref)

    heads = range(GLA_HEADS)
    ks = [slice(h * GLA_DK, (h + 1) * GLA_DK) for h in heads]
    vs = [slice(h * GLA_DV, (h + 1) * GLA_DV) for h in heads]
    x = _bdot(lr_ref[...], w2_ref[...]) + bias_ref[...]
    la = (jnp.minimum(x, 0.0) - jnp.log1p(jnp.exp(-jnp.abs(x)))) / GLA_TAU
    ex = jnp.exp(_mask_dot(em_ref[...], la))
    e_b = ex[0:CH]
    e_rest = ex[CH:2 * CH]
    q = q_ref[...] * GLA_DK ** -0.5
    k = k_ref[...]
    v = [v_ref[:, s] for s in vs]
    scores = [_bdot_nt(q[:, s], k[:, s]) * pm_ref[GLA_LEVELS] for s in ks]
    for lvl in range(GLA_LEVELS):
        qe = q * ex[(2 + 2 * lvl) * CH:(3 + 2 * lvl) * CH]
        ke = k * ex[(3 + 2 * lvl) * CH:(4 + 2 * lvl) * CH]
        pm = pm_ref[lvl]
        scores = [scores[h] + _bdot_nt(qe[:, ks[h]], ke[:, ks[h]]) * pm for h in heads]
    st = [s_ref[h] for h in heads]
    qb = q * e_b
    kr = k * e_rest
    o = [_bdot(scores[h], v[h]) + _bdot_nt(qb[:, ks[h]], st[h]) for h in heads]
    upd = [_bdot_tn(v[h], kr[:, ks[h]]) for h in heads]
    for h in heads:
        s_ref[h] = st[h] * e_b[CH - 1:CH, ks[h]] + upd[h]
        oh = o[h] * lax.rsqrt(jnp.mean(o[h] * o[h], axis=-1, keepdims=True) + EPS) * gain_ref[:, vs[h]]
        o_ref[:, vs[h]] = (oh * _silu(g_ref[:, vs[h]])).astype(o_ref.dtype)


def _gla(proj, w2p, bias, norm_gain, batch, lp):
    m = proj.shape[0]
    nc = lp // CH
    em, pm = _gla_consts()
    qk_w = GLA_HEADS * GLA_DK
    const = lambda shape: pl.BlockSpec(shape, lambda b, c: (0,) * len(shape))
    return pl.pallas_call(
        _gla_kernel,
        grid=(batch, nc),
        in_specs=[pl.BlockSpec((CH, qk_w), lambda b, c: (b * nc + c, COL_GLA_QK // qk_w)),
                  pl.BlockSpec((CH, qk_w), lambda b, c: (b * nc + c, COL_GLA_QK // qk_w + 1)),
                  pl.BlockSpec((CH, MIX_W), lambda b, c: (b * nc + c, COL_GLA_VG // MIX_W)),
                  pl.BlockSpec((CH, MIX_W), lambda b, c: (b * nc + c, COL_GLA_VG // MIX_W + 1)),
                  pl.BlockSpec((CH, LANE), lambda b, c: (b * nc + c, COL_GLA_LR // LANE)),
                  const((LANE, qk_w)), const((1, qk_w)), const((1, MIX_W)),
                  const(em.shape), const(pm.shape)],
        out_specs=pl.BlockSpec((CH, MIX_W), lambda b, c: (b * nc + c, 0)),
        out_shape=jax.ShapeDtypeStruct((m, MIX_W), BF16),
        scratch_shapes=[pltpu.VMEM((GLA_HEADS, GLA_DV, GLA_DK), F32)],
        compiler_params=_cparams(("parallel", "arbitrary")),
        name="gla",
    )(proj, proj, proj, proj, proj, w2p, bias.reshape(1, -1), norm_gain.reshape(1, MIX_W), em, pm)


def _gdn_kernel(q_ref, k_ref, v_ref, z_ref, ab_ref, cq_ref, ck_ref, cv_ref, alog_ref, dtb_ref,
                gain_ref, tri_ref, blk_ref, o_ref, s_ref, carry_ref):
    @pl.when(pl.program_id(1) == 0)
    def _():
        s_ref[...] = jnp.zeros_like(s_ref)
        carry_ref[...] = jnp.zeros_like(carry_ref)

    def conv_silu(x_ref, c_ref, slot):
        x = x_ref[...]
        ext = jnp.concatenate([carry_ref[slot], x], axis=0)
        carry_ref[slot] = x[CH - 8:, :]
        c = c_ref[...]
        y = c[3:4] * x
        for t in range(1, GDN_CONV):
            y = y + c[3 - t:4 - t] * ext[8 - t:8 - t + CH, :]
        return _silu(y)

    def l2norm(t):
        return t * lax.rsqrt(jnp.sum(t * t, axis=-1, keepdims=True) + EPS)

    qc = conv_silu(q_ref, cq_ref, 0)
    kc = conv_silu(k_ref, ck_ref, 1)
    vc = conv_silu(v_ref, cv_ref, 2)
    ab = ab_ref[...]
    g_t = -jnp.exp(alog_ref[...]) * _softplus(ab + dtb_ref[...])
    beta_t = _sigmoid(ab)
    rep = lambda t, idx: jnp.broadcast_to(t[:, idx:idx + 1], (CH, LANE))
    heads = range(GDN_HEADS)
    hs = [slice(h * GDN_D, (h + 1) * GDN_D) for h in heads]
    gcum_all = _mask_dot(tri_ref[...], jnp.concatenate([rep(g_t, h) for h in heads], axis=1))

    same_blk = blk_ref[...]
    ri = lax.broadcasted_iota(jnp.int32, (CH, CH), 0)
    ci = lax.broadcasted_iota(jnp.int32, (CH, CH), 1)
    eye = (ri == ci).astype(F32)
    q = [l2norm(qc[:, s]) * GDN_D ** -0.5 for s in hs]
    k = [l2norm(kc[:, s]) for s in hs]
    v = [vc[:, s] for s in hs]
    beta = [rep(beta_t, GDN_HEADS + h) for h in heads]
    gcum = [gcum_all[:, s] for s in hs]
    grow = [jnp.transpose(g)[:1, :CH] for g in gcum]
    decay = [jnp.where(ri >= ci, jnp.exp(jnp.minimum(gcum[h][:, :1] - grow[h], 0.0)), 0.0) for h in heads]
    kb = [k[h] * beta[h] for h in heads]
    kq = [_bdot_nt(jnp.concatenate([kb[h], q[h]], axis=0), k[h]) for h in heads]
    a_mat = [jnp.where(ri > ci, kq[h][:CH] * decay[h], 0.0) for h in heads]
    attn = [kq[h][CH:] * decay[h] for h in heads]
    t_inv = _inv_unit_lower(a_mat, same_blk, eye)
    e_g = [jnp.exp(g) for g in gcum]
    sol = [_bdot(t_inv[h], jnp.concatenate([v[h] * beta[h], kb[h] * e_g[h]], axis=1)) for h in heads]
    s = [s_ref[h] for h in heads]
    m = [_bdot(jnp.concatenate([sol[h][:, GDN_D:], q[h] * e_g[h]], axis=0), s[h]) for h in heads]
    v_new = [sol[h][:, :GDN_D] - m[h][:CH] for h in heads]
    o = [m[h][CH:] + _bdot(attn[h], v_new[h]) for h in heads]
    g_last = [g[CH - 1:CH, :] for g in gcum]
    upd = [_bdot_tn(k[h] * jnp.exp(g_last[h] - gcum[h]), v_new[h]) for h in heads]
    gain = gain_ref[...]
    for h in heads:
        s_ref[h] = s[h] * jnp.exp(g_last[h]) + upd[h]
        oh = o[h] * lax.rsqrt(jnp.mean(o[h] * o[h], axis=-1, keepdims=True) + EPS) * gain
        o_ref[:, hs[h]] = (oh * _silu(z_ref[:, hs[h]])).astype(o_ref.dtype)


def _gdn(proj, conv_w, a_log, dt_bias, norm_gain, batch, lp):
    m = proj.shape[0]
    nc = lp // CH
    nh = GDN_HEADS
    pad = lambda t: jnp.pad(t.reshape(1, nh), ((0, 0), (0, LANE - nh)))
    part = lambda p: pl.BlockSpec((CH, MIX_W), lambda b, c: (b * nc + c, COL_GDN // MIX_W + p))
    cpart = lambda p: pl.BlockSpec((GDN_CONV, MIX_W), lambda b, c: (0, p))
    const = lambda shape: pl.BlockSpec(shape, lambda b, c: (0,) * len(shape))
    return pl.pallas_call(
        _gdn_kernel,
        grid=(batch, nc),
        in_specs=[part(0), part(1), part(2), part(3),
                  pl.BlockSpec((CH, LANE), lambda b, c: (b * nc + c, COL_GDN_AB // LANE)),
                  cpart(0), cpart(1), cpart(2),
                  const((1, LANE)), const((1, LANE)), const((1, GDN_D)),
                  const((CH, CH)), const((CH, CH))],
        out_specs=pl.BlockSpec((CH, MIX_W), lambda b, c: (b * nc + c, 0)),
        out_shape=jax.ShapeDtypeStruct((m, MIX_W), BF16),
        scratch_shapes=[pltpu.VMEM((nh, GDN_D, GDN_D), F32), pltpu.VMEM((3, 8, MIX_W), F32)],
        compiler_params=_cparams(("parallel", "arbitrary")),
        name="gdn",
    )(proj, proj, proj, proj, proj, conv_w, conv_w, conv_w, pad(a_log), pad(dt_bias),
      norm_gain.reshape(1, GDN_D), _const_tri(), _const_same_blk())


def _rwkv_prep_kernel(sr_ref, sk_ref, sv_ref, swa_ref, sg_ref, mu_ref, w0_ref, a0_ref, kk_ref, ka_ref,
                      w2_ref, a2_ref, g2_ref, seg_ref, segt_ref,
                      r_ref, lw_ref, k_ref, v_ref, na_ref, bb_ref, g_ref, carry_ref):
    rb = pl.program_id(1)

    @pl.when(rb == 0)
    def _():
        carry_ref[...] = jnp.zeros_like(carry_ref)

    x = jnp.concatenate([sr_ref[...], sk_ref[...], sv_ref[...], swa_ref[...], sg_ref[...]], axis=1)
    ext = jnp.concatenate([carry_ref[...], x], axis=0)
    carry_ref[...] = x[ROWS - 8:, :]
    prev = ext[7:7 + ROWS, :]
    s = x + (prev - x) * mu_ref[...]
    r = s[:, :MIX_W]
    k = s[:, MIX_W:2 * MIX_W]
    v = s[:, 2 * MIX_W:3 * MIX_W]
    wa_src = s[:, RWKV_LR:RWKV_LR + LANE]
    g_src = s[:, RWKV_LR + LANE:]
    lw_ref[...] = -RWKV_DECAY_SCALE * _sigmoid(w0_ref[...] + _bdot(jnp.tanh(wa_src), w2_ref[...]))
    a = _sigmoid(a0_ref[...] + _bdot(wa_src, a2_ref[...]))
    g_ref[...] = _bdot(_sigmoid(g_src), g2_ref[...])
    kk = k * kk_ref[...]
    sq = kk * kk
    sq_hi = sq.astype(BF16)
    sq_lo = (sq - sq_hi.astype(F32)).astype(BF16)
    seg = seg_ref[...]
    hsum = (jnp.dot(sq_hi, seg, preferred_element_type=F32)
            + jnp.dot(sq_lo, seg, preferred_element_type=F32))
    inv = lax.rsqrt(hsum + EPS)
    inv_hi = inv.astype(BF16)
    inv_lo = (inv - inv_hi.astype(F32)).astype(BF16)
    segt = segt_ref[...]
    inv_full = (jnp.dot(inv_hi, segt, preferred_element_type=F32)
                + jnp.dot(inv_lo, segt, preferred_element_type=F32))
    kk = kk * inv_full
    r_ref[...] = r
    k_ref[...] = k * (1.0 + (a - 1.0) * ka_ref[...])
    v_ref[...] = v
    na_ref[...] = -kk
    bb_ref[...] = kk * a


def _rwkv_scan_kernel(r_ref, lw_ref, k_ref, v_ref, na_ref, bb_ref, g_ref, rk_ref, lnw_ref, lnb_ref,
                      cm_ref, blk_ref, o_ref, h_ref):
    @pl.when(pl.program_id(1) == 0)
    def _():
        h_ref[...] = jnp.zeros_like(h_ref)

    n = RWKV_N
    npair = MIX_W // LANE
    ri = lax.broadcasted_iota(jnp.int32, (CH, CH), 0)
    ci = lax.broadcasted_iota(jnp.int32, (CH, CH), 1)
    eye = (ri == ci).astype(F32)
    strict = ri > ci
    ri2 = lax.broadcasted_iota(jnp.int32, (CH, LANE), 0)
    ci2 = lax.broadcasted_iota(jnp.int32, (CH, LANE), 1) % CH
    incl2 = ri2 >= ci2
    lo = lax.broadcasted_iota(jnp.int32, (1, LANE), 1) < n
    vi = lax.broadcasted_iota(jnp.int32, (LANE, LANE), 0) // n
    ki = lax.broadcasted_iota(jnp.int32, (LANE, LANE), 1) // n
    same_head = (vi == ki).astype(F32)
    same_blk = blk_ref[...]

    r, lw, k, v = r_ref[...], lw_ref[...], k_ref[...], v_ref[...]
    na, bb = na_ref[...], bb_ref[...]
    cums = _mask_dot(cm_ref[...], lw)
    cum, rest = cums[:CH], cums[CH:]
    e_cum = jnp.exp(cum)
    e_neg = jnp.exp(-cum)
    e_rest = jnp.exp(rest)
    rt = r * e_cum
    at = na * jnp.exp(cum - lw)
    bt = bb * e_neg
    kt = k * e_neg
    bh = bb * e_rest
    kh = k * e_rest

    pairs = [slice(p * LANE, (p + 1) * LANE) for p in range(npair)]
    pm = []
    for ps in pairs:
        a_p, r_p = at[:, ps], rt[:, ps]
        lhs = jnp.concatenate([jnp.where(lo, a_p, 0.0), jnp.where(lo, r_p, 0.0),
                               jnp.where(lo, 0.0, a_p), jnp.where(lo, 0.0, r_p)], axis=0)
        pm.append(_bdot_nt(lhs, jnp.concatenate([bt[:, ps], kt[:, ps]], axis=0)))
    heads = [(p, s) for p in range(npair) for s in range(2)]
    a_ab = [jnp.where(strict, pm[p][2 * s * CH:(2 * s + 1) * CH, :CH], 0.0) for p, s in heads]
    a_ak = [jnp.where(strict, pm[p][2 * s * CH:(2 * s + 1) * CH, CH:], 0.0) for p, s in heads]
    a_rbk = [jnp.where(incl2, pm[p][(2 * s + 1) * CH:(2 * s + 2) * CH, :], 0.0) for p, s in heads]
    t_inv = _inv_unit_lower([-a for a in a_ab], same_blk, eye)
    akv = [_bdot(a_ak[i], v[:, pairs[p]]) for i, (p, s) in enumerate(heads)]
    sol = [_bdot(t_inv[i], jnp.concatenate([at[:, pairs[p]], akv[i]], axis=1))
           for i, (p, s) in enumerate(heads)]
    sol = [jnp.where(jnp.concatenate([lo, lo], axis=1), sol[2 * p], sol[2 * p + 1]) for p in range(npair)]
    ht = [h_ref[p] for p in range(npair)]
    m = [_bdot_nt(jnp.concatenate([sol[p][:, :LANE], rt[:, pairs[p]]], axis=0), ht[p]) for p in range(npair)]
    uv = [jnp.concatenate([m[p][:CH] + sol[p][:, LANE:], v[:, pairs[p]]], axis=0) for p in range(npair)]
    yy = [_bdot(a_rbk[i], uv[p]) for i, (p, s) in enumerate(heads)]
    upd = [_bdot_tn(uv[p], jnp.concatenate([bh[:, pairs[p]], kh[:, pairs[p]]], axis=0)) for p in range(npair)]
    for p, ps in enumerate(pairs):
        h_ref[p] = ht[p] * e_cum[CH - 1:CH, ps] + upd[p] * same_head
        y = m[p][CH:] + jnp.where(lo, yy[2 * p], yy[2 * p + 1])

        def head_sum(t):
            s0 = jnp.sum(jnp.where(lo, t, 0.0), axis=-1, keepdims=True)
            s1 = jnp.sum(jnp.where(lo, 0.0, t), axis=-1, keepdims=True)
            return jnp.where(lo, s0, s1)

        y = y - head_sum(y) * (1.0 / n)
        y = y * lax.rsqrt(head_sum(y * y) * (1.0 / n) + RWKV_GN_EPS)
        y = y * lnw_ref[:, ps] + lnb_ref[:, ps]
        bonus = head_sum(r[:, ps] * k[:, ps] * rk_ref[:, ps])
        o_ref[:, ps] = ((y + bonus * v[:, ps]) * g_ref[:, ps]).astype(o_ref.dtype)


def _rwkv(proj, mu_p, w0, w2p, a0, a2p, g2p, kk, ka, rk, ln_w, ln_b, batch, lp):
    m = proj.shape[0]
    nr = lp // ROWS
    c = np.arange(MIX_W)[:, None] // RWKV_N == np.arange(LANE)[None, :]
    seg = jnp.asarray(c.astype(np.float32), BF16)
    segt = jnp.asarray(c.T.astype(np.float32), BF16)
    vec = lambda t: t.reshape(1, -1)
    const2 = lambda shape: pl.BlockSpec(shape, lambda b, r: (0, 0))
    row_w = pl.BlockSpec((ROWS, MIX_W), lambda b, r: (b * nr + r, 0))
    wide = jax.ShapeDtypeStruct((m, MIX_W), F32)
    src = lambda width, col: pl.BlockSpec((ROWS, width), lambda b, r: (b * nr + r, col // width))
    r_, lw, k_, v_, na, bb, g = pl.pallas_call(
        _rwkv_prep_kernel,
        grid=(batch, nr),
        in_specs=[src(MIX_W, COL_RWKV), src(MIX_W, COL_RWKV + MIX_W), src(MIX_W, COL_RWKV + 2 * MIX_W),
                  src(LANE, COL_RWKV_WA), src(2 * LANE, COL_RWKV_G),
                  const2((1, RWKV_PW)), const2((1, MIX_W)), const2((1, MIX_W)), const2((1, MIX_W)),
                  const2((1, MIX_W)), const2((LANE, MIX_W)), const2((LANE, MIX_W)),
                  const2((2 * LANE, MIX_W)), const2((MIX_W, LANE)), const2((LANE, MIX_W))],
        out_specs=[row_w] * 7,
        out_shape=[wide] * 7,
        scratch_shapes=[pltpu.VMEM((8, RWKV_PW), F32)],
        compiler_params=_cparams(("parallel", "arbitrary")),
        name="rwkv_prep",
    )(proj, proj, proj, proj, proj, vec(mu_p), vec(w0), vec(a0), vec(kk), vec(ka), w2p, a2p, g2p, seg, segt)

    i, j = _np_masks()
    cm = jnp.asarray(np.concatenate([(j <= i), (j > i)], axis=0).astype(np.float32), BF16)
    nc = lp // CH
    chunk = pl.BlockSpec((CH, MIX_W), lambda b, c: (b * nc + c, 0))
    return pl.pallas_call(
        _rwkv_scan_kernel,
        grid=(batch, nc),
        in_specs=[chunk] * 7 + [const2((1, MIX_W))] * 3 + [const2((2 * CH, CH)), const2((CH, CH))],
        out_specs=chunk,
        out_shape=jax.ShapeDtypeStruct((m, MIX_W), BF16),
        scratch_shapes=[pltpu.VMEM((MIX_W // LANE, LANE, LANE), F32)],
        compiler_params=_cparams(("parallel", "arbitrary")),
        name="rwkv_scan",
    )(r_, lw, k_, v_, na, bb, g, vec(rk), vec(ln_w), vec(ln_b), cm, _const_same_blk())


def _pad_cols(w, width):
    return jnp.pad(w, ((0, 0),) * (w.ndim - 1) + ((0, width - w.shape[-1]),))


def _pad_rows(w, rows, offset=0):
    return jnp.pad(w, ((offset, rows - offset - w.shape[0]), (0, 0)))


def _reorder_w_in(w_in):
    ret_w = 4 * MIX_W
    gla = ret_w
    gla_v = gla + 2 * GLA_HEADS * GLA_DK
    gla_lr = gla_v + 2 * GLA_HEADS * GLA_DV
    gdn = gla_lr + GLA_RANK
    gdn_ab = gdn + 4 * MIX_W
    rwkv = gdn_ab + 2 * GDN_HEADS
    rwkv_wa = rwkv + 3 * MIX_W
    rwkv_g = rwkv_wa + RWKV_W_RANK + RWKV_A_RANK
    w = w_in.astype(BF16)
    pieces = [w[..., :ret_w], w[..., gdn:gdn_ab], w[..., rwkv:rwkv_wa], w[..., gla_v:gla_lr],
              w[..., gla:gla_v], _pad_cols(w[..., gdn_ab:rwkv], LANE), w[..., rwkv_wa:rwkv_g],
              _pad_cols(w[..., rwkv_g:], 2 * LANE), _pad_cols(w[..., gla_lr:gdn], PROJ_W - COL_GLA_LR)]
    out = jnp.concatenate(pieces, axis=-1)
    assert out.shape[-1] == PROJ_W, out.shape
    return out


def _mixer_sublayer(x, h, lyr, big, layer, cos, sin, batch, lp, next_g):
    proj, w_gate_half = _matmul(h, big["w_proj"], layer, 1040, PROJ_TN, F32, "proj", (big["w_gate"], layer), 0.5)

    o_a = _retention(proj, cos, sin, lyr["ret_norm"], batch, lp)
    o_b = _gla(proj, _pad_rows(lyr["gla_w2"], LANE).astype(BF16), lyr["gla_b"], lyr["gla_norm"],
               batch, lp)
    o_c = _gdn(proj, lyr["gdn_conv"], lyr["gdn_a_log"], lyr["gdn_dt_bias"], lyr["gdn_norm"], batch, lp)
    mu_p = jnp.pad(lyr["rwkv_mu"], (0, RWKV_PW - lyr["rwkv_mu"].shape[0]))
    w2p = _pad_rows(lyr["rwkv_w2"], LANE).astype(BF16)
    a2p = _pad_rows(lyr["rwkv_a2"], LANE, RWKV_W_RANK).astype(BF16)
    g2p = _pad_rows(lyr["rwkv_g2"], 2 * LANE).astype(BF16)
    o_d = _rwkv(proj, mu_p, lyr["rwkv_w0"], w2p, lyr["rwkv_a0"], a2p, g2p, lyr["rwkv_kk"],
                lyr["rwkv_ka"], lyr["rwkv_rk"], lyr["rwkv_ln_w"], lyr["rwkv_ln_b"], batch, lp)

    o4 = jnp.stack([o_a, o_b, o_c, o_d], axis=0)
    merged, w_up = _gate_merge(h, o4, w_gate_half, big["w_branch"], (big["w_up"], layer))
    z = _matmul(merged, big["w_out"], layer, 1040, 1024, F32, "w_out")
    x, h = _norm_residual(z, x, lyr["post_mix"], next_g)
    return x, h, w_up


def _ffn_sublayer(x, h, w_up, lyr, big, layer, lp, next_g):
    act, w_down = _ffn_up(h, w_up, big["ffn_conv"], lp, (big["w_down"], layer))
    z = _matmul(act, w_down, None, 520, 512, F32, "ffn_down")
    return _norm_residual(z, x, lyr["post_ffn"], next_g)


_LAYER_KEYS = ("pre_mix", "w_in", "ret_norm", "gla_w2", "gla_b", "gla_norm", "gdn_conv", "gdn_a_log",
               "gdn_dt_bias", "gdn_norm", "rwkv_mu", "rwkv_w0", "rwkv_w2", "rwkv_a0", "rwkv_a2", "rwkv_g2",
               "rwkv_kk", "rwkv_ka", "rwkv_rk", "rwkv_ln_w", "rwkv_ln_b", "w_branch", "w_gate", "w_out",
               "post_mix", "pre_ffn", "w_up", "ffn_conv", "w_down", "post_ffn")
_BIG_KEYS = ("w_in", "w_gate", "w_branch", "w_out", "w_up", "w_down", "ffn_conv")


def _big_weights(params):
    big = {k: params[k] for k in ("w_gate", "w_up", "w_down")}
    big["w_branch"] = (0.5 * params["w_branch"]).astype(BF16)
    half_a = jnp.concatenate([jnp.full((D_FF,), 0.5, F32), jnp.ones((D_FF,), F32)])
    big["ffn_conv"] = params["ffn_conv"] * half_a
    big["w_out"] = params["w_out"].astype(BF16)
    big["w_proj"] = _reorder_w_in(params["w_in"])
    return big


def _trunk(x, meta, params):
    batch, seq, d = x.shape
    depth = params["pre_mix"].shape[0]
    l = N_META + seq
    lp = -(-l // ROWS) * ROWS
    hcat = jnp.concatenate([jnp.broadcast_to(meta.astype(x.dtype)[None], (batch, N_META, d)), x,
                            jnp.zeros((batch, lp - l, d), x.dtype)], axis=1)
    xr = hcat.reshape(batch * lp, d)

    half = RET_D // 2
    inv_freq = ROPE_BASE ** (-jnp.arange(half, dtype=F32) / half)
    ang = jnp.arange(lp, dtype=F32)[:, None] * inv_freq[None, :]
    cos, sin = jnp.cos(ang), jnp.sin(ang)

    big = _big_weights(params)
    layers = [{k: params[k][i] for k in _LAYER_KEYS if k not in _BIG_KEYS} for i in range(depth)]
    h = _prenorm(xr, layers[0]["pre_mix"])
    for i, lyr in enumerate(layers):
        xr, h, w_up = _mixer_sublayer(xr, h, lyr, big, i, cos, sin, batch, lp, lyr["pre_ffn"])
        next_g = layers[i + 1]["pre_mix"] if i + 1 < depth else None
        xr, h = _ffn_sublayer(xr, h, w_up, lyr, big, i, lp, next_g)
    return xr.reshape(batch, lp, d)[:, N_META:l]


def kernel(x, meta, pre_mix, w_in, ret_norm, gla_w2, gla_b, gla_norm, gdn_conv, gdn_a_log, gdn_dt_bias,
           gdn_norm, rwkv_mu, rwkv_w0, rwkv_w2, rwkv_a0, rwkv_a2, rwkv_g2, rwkv_kk, rwkv_ka, rwkv_rk,
           rwkv_ln_w, rwkv_ln_b, w_branch, w_gate, w_out, post_mix, pre_ffn, w_up, ffn_conv, w_down, post_ffn):
    params = dict(zip(_LAYER_KEYS, (pre_mix, w_in, ret_norm, gla_w2, gla_b, gla_norm, gdn_conv, gdn_a_log,
                                    gdn_dt_bias, gdn_norm, rwkv_mu, rwkv_w0, rwkv_w2, rwkv_a0, rwkv_a2,
                                    rwkv_g2, rwkv_kk, rwkv_ka, rwkv_rk, rwkv_ln_w, rwkv_ln_b, w_branch,
                                    w_gate, w_out, post_mix, pre_ffn, w_up, ffn_conv, w_down, post_ffn)))
    return _trunk(x, meta, params)
```
